```python
import jax, jax.numpy as jnp
from jax import lax
import numpy as np

D_MODEL = 1024
BATCH = 16
SEQ = 2048
DEPTH = 4

N_MIXERS = 3
HEAD_DIM = 64
N_HEADS = D_MODEL // HEAD_DIM
MOBA_BLOCK = 256
MOBA_TOPK = 3
MOBA_QCHUNK = 64
SWA_WINDOW = 128
SWA_KV_HEADS = 4
SB_QBLOCK = 128
N_GROUPS = 4
EXPERTS_PER_GROUP = 8
N_EXPERTS = N_GROUPS * EXPERTS_PER_GROUP
TOPK_IN_GROUP = 2
D_EXPERT = 512
MOE_BLOCK = 256
PLE_DIM = 256
DEEPNORM_ALPHA = (2.0 * DEPTH) ** 0.25
DEEPNORM_BETA = (8.0 * DEPTH) ** -0.25
LN_EPS = 1e-5

kernel_name = 'hybrid_moba_swa_stickbreak_hmoe_deepnorm'


def _n_layers_of(mixer):
    return len(range(mixer, DEPTH, N_MIXERS))


def _alibi_slopes(n_heads):
    return jnp.asarray(np.array([2.0 ** (-8.0 * (h + 1) / n_heads) for h in range(n_heads)], dtype=np.float32))


def layer_norm(x, g, b):
    xf = x.astype(jnp.float32)
    mu = xf.mean(-1, keepdims=True)
    var = jnp.square(xf - mu).mean(-1, keepdims=True)
    y = (xf - mu) * lax.rsqrt(var + LN_EPS) * g.astype(jnp.float32) + b.astype(jnp.float32)
    return y.astype(x.dtype)


def moba_attention(q, k, v):
    bsz, seq, n_h, hd = q.shape
    s_pad = -(-seq // MOBA_BLOCK) * MOBA_BLOCK
    padw = ((0, 0), (0, s_pad - seq), (0, 0), (0, 0))
    qh = jnp.pad(q, padw).transpose(0, 2, 1, 3)
    kb = jnp.pad(k, padw).transpose(0, 2, 1, 3).reshape(bsz, n_h, -1, MOBA_BLOCK, hd)
    vb = jnp.pad(v, padw).transpose(0, 2, 1, 3).reshape(bsz, n_h, -1, MOBA_BLOCK, hd)
    n_blk = s_pad // MOBA_BLOCK
    k_sel = min(MOBA_TOPK, n_blk)
    scale = hd ** -0.5
    slopes = _alibi_slopes(n_h)
    q_blk = jnp.arange(s_pad) // MOBA_BLOCK
    k_mean = kb.astype(jnp.float32).mean(axis=3)
    gate = jnp.einsum('bhqd,bhnd->bhqn', qh.astype(jnp.float32), k_mean)
    gate = jnp.where(jnp.arange(n_blk)[None, :] < q_blk[:, None], gate, -jnp.inf)
    _, sel = lax.top_k(gate, k_sel)
    sel_valid = sel < q_blk[None, None, :, None]
    n_chunk = s_pad // MOBA_QCHUNK
    h_idx = jnp.arange(n_h)[:, None, None]
    blk_off = jnp.arange(MOBA_BLOCK)

    def chunk(idx):
        b = idx // n_chunk
        q0 = (idx % n_chunk) * MOBA_QCHUNK
        own = q0 // MOBA_BLOCK
        qc = lax.dynamic_slice_in_dim(qh[b], q0, MOBA_QCHUNK, axis=1)
        sel_c = lax.dynamic_slice_in_dim(sel[b], q0, MOBA_QCHUNK, axis=1)
        val_c = lax.dynamic_slice_in_dim(sel_valid[b], q0, MOBA_QCHUNK, axis=1)
        t_pos = q0 + jnp.arange(MOBA_QCHUNK)
        k_own = kb[b, :, own]
        v_own = vb[b, :, own]
        d_own = t_pos[:, None] - (own * MOBA_BLOCK + blk_off)[None, :]
        l_own = jnp.einsum('hqd,hkd->hqk', qc, k_own, preferred_element_type=jnp.float32) * scale - slopes[:, None, None] * d_own
        l_own = jnp.where(d_own >= 0, l_own, -jnp.inf)
        k_g = kb[b, h_idx, sel_c]
        v_g = vb[b, h_idx, sel_c]
        d_sel = t_pos[None, :, None, None] - (sel_c[..., None] * MOBA_BLOCK + blk_off)
        l_sel = jnp.einsum('hqd,hqnkd->hqnk', qc, k_g, preferred_element_type=jnp.float32) * scale - slopes[:, None, None, None] * d_sel
        l_sel = jnp.where(val_c[..., None], l_sel, -jnp.inf)
        logits = jnp.concatenate([l_own, l_sel.reshape(n_h, MOBA_QCHUNK, k_sel * MOBA_BLOCK)], axis=-1)
        w = jax.nn.softmax(logits, axis=-1).astype(v.dtype)
        w_sel = w[..., MOBA_BLOCK:].reshape(n_h, MOBA_QCHUNK, k_sel, MOBA_BLOCK)
        return jnp.einsum('hqk,hkd->hqd', w[..., :MOBA_BLOCK], v_own) + jnp.einsum('hqnk,hqnkd->hqd', w_sel, v_g)

    out = lax.map(chunk, jnp.arange(bsz * n_chunk))
    out = out.reshape(bsz, n_chunk, n_h, MOBA_QCHUNK, hd).transpose(0, 1, 3, 2, 4).reshape(bsz, s_pad, n_h, hd)
    return out[:, :seq]


def swa_gqa_attention(q, k, v, sinks):
    bsz, seq, n_h, hd = q.shape
    n_kv = k.shape[2]
    grp = n_h // n_kv
    win = SWA_WINDOW
    n_b = seq // win
    qb = q.reshape(bsz, n_b, win, n_kv, grp, hd)
    kb = k.reshape(bsz, n_b, win, n_kv, hd)
    vb = v.reshape(bsz, n_b, win, n_kv, hd)
    shift = ((0, 0), (1, 0), (0, 0), (0, 0), (0, 0))
    k_band = jnp.concatenate([jnp.pad(kb, shift)[:, :-1], kb], axis=2)
    v_band = jnp.concatenate([jnp.pad(vb, shift)[:, :-1], vb], axis=2)
    logits = jnp.einsum('bnqkgd,bnskd->bkgnqs', qb, k_band, preferred_element_type=jnp.float32) * (hd ** -0.5)
    dist = (jnp.arange(win)[:, None] + win) - jnp.arange(2 * win)[None, :]
    first = (jnp.arange(n_b) == 0)[:, None, None] & (jnp.arange(2 * win) < win)[None, None, :]
    allowed = ((dist >= 0) & (dist < win))[None] & ~first
    slopes = _alibi_slopes(n_h).reshape(1, n_kv, grp, 1, 1, 1)
    logits = jnp.where(allowed, logits - slopes * dist, -jnp.inf)
    sink = jnp.broadcast_to(sinks.astype(jnp.float32).reshape(1, n_kv, grp, 1, 1, 1), logits.shape[:-1] + (1,))
    w = jax.nn.softmax(jnp.concatenate([logits, sink], axis=-1), axis=-1)[..., :-1]
    o = jnp.einsum('bkgnqs,bnskd->bnqkgd', w.astype(v.dtype), v_band)
    return o.reshape(bsz, seq, n_h, hd)


def stick_breaking_attention(q, k, v):
    bsz, seq, n_h, hd = q.shape
    n_b = seq // SB_QBLOCK
    qh = q.transpose(0, 2, 1, 3)
    kh = k.transpose(0, 2, 1, 3)
    vh = v.transpose(0, 2, 1, 3)
    s_pos = jnp.arange(seq)
    scale = hd ** -0.5

    def block(c):
        q0 = c * SB_QBLOCK
        qc = lax.dynamic_slice_in_dim(qh, q0, SB_QBLOCK, axis=2)
        z = jnp.einsum('bhqd,bhsd->bhqs', qc, kh, preferred_element_type=jnp.float32) * scale
        strict = s_pos[None, :] < (q0 + jnp.arange(SB_QBLOCK))[:, None]
        log_keep = jnp.where(strict, jax.nn.log_sigmoid(-z), 0.0)
        after = lax.cumsum(log_keep, axis=3, reverse=True) - log_keep
        a = jnp.where(strict, jnp.exp(jax.nn.log_sigmoid(z) + after), 0.0)
        return jnp.einsum('bhqs,bhsd->bhqd', a.astype(vh.dtype), vh)

    out = lax.map(block, jnp.arange(n_b))
    return out.transpose(1, 0, 3, 2, 4).reshape(bsz, seq, n_h, hd)


def hierarchical_moe(x, w_grp, b_grp, w_exp, b_exp, w_e_gate, w_e_up, w_e_down):
    bsz, seq, d = x.shape
    xt = x.reshape(-1, d)
    n_tok = xt.shape[0]
    grp_prob = jax.nn.softmax((xt @ w_grp).astype(jnp.float32) + b_grp.astype(jnp.float32), axis=-1)
    g_p, g_idx = lax.top_k(grp_prob, 1)
    e_logits = ((xt @ w_exp).astype(jnp.float32) + b_exp.astype(jnp.float32)).reshape(n_tok, N_GROUPS, EXPERTS_PER_GROUP)
    in_grp = jnp.take_along_axis(e_logits, g_idx[:, :, None], axis=1)[:, 0]
    e_p, e_loc = lax.top_k(jax.nn.softmax(in_grp, axis=-1), TOPK_IN_GROUP)
    gate = g_p * (e_p / e_p.sum(-1, keepdims=True))
    e_glob = g_idx * EXPERTS_PER_GROUP + e_loc
    n_asg = n_tok * TOPK_IN_GROUP
    flat_e = e_glob.reshape(-1)
    flat_w = gate.reshape(-1)
    flat_tok = jnp.repeat(jnp.arange(n_tok), TOPK_IN_GROUP)
    order = jnp.argsort(flat_e)
    se = flat_e[order]
    counts = jnp.bincount(flat_e, length=N_EXPERTS)
    start = jnp.cumsum(counts) - counts
    pcounts = (counts + MOE_BLOCK - 1) // MOE_BLOCK * MOE_BLOCK
    pend = jnp.cumsum(pcounts)
    dest = (pend - pcounts)[se] + jnp.arange(n_asg) - start[se]
    n_pad = -(-n_asg // MOE_BLOCK) * MOE_BLOCK + N_EXPERTS * MOE_BLOCK
    n_blocks = n_pad // MOE_BLOCK
    tok_pad = jnp.zeros((n_pad,), jnp.int32).at[dest].set(flat_tok[order])
    w_pad = jnp.zeros((n_pad,), jnp.float32).at[dest].set(flat_w[order])
    block_e = jnp.minimum(jnp.searchsorted(pend, jnp.arange(n_blocks) * MOE_BLOCK, side='right'), N_EXPERTS - 1)

    def expert_block(args):
        xb, e = args
        h = jax.nn.silu(xb @ w_e_gate[e]) * (xb @ w_e_up[e])
        return h @ w_e_down[e]

    y = lax.map(expert_block, (xt[tok_pad].reshape(n_blocks, MOE_BLOCK, d), block_e))
    y = y.reshape(n_pad, d) * w_pad[:, None].astype(y.dtype)
    out = jnp.zeros_like(xt).at[tok_pad].add(y)
    return out.reshape(bsz, seq, d)


def setup_inputs(seed: int = 0) -> dict:
    key = jax.random.key(seed)
    ks = jax.random.split(key, 32)
    f32 = jnp.float32
    d = D_MODEL
    n_a, n_b, n_c = _n_layers_of(0), _n_layers_of(1), _n_layers_of(2)
    kv_w = SWA_KV_HEADS * HEAD_DIM

    def nrm(k, shape, scale):
        return jax.random.normal(k, shape, f32) * scale

    return {
        'x': nrm(ks[0], (BATCH, SEQ, d), 1.0),
        'p': nrm(ks[1], (DEPTH, BATCH, SEQ, PLE_DIM), 1.0),
        'w_qkv_a': nrm(ks[2], (n_a, d, 3 * d), d ** -0.5),
        'w_o_a': nrm(ks[3], (n_a, d, d), d ** -0.5 * DEEPNORM_BETA),
        'w_qkv_b': nrm(ks[4], (n_b, d, d + 2 * kv_w), d ** -0.5),
        'w_o_b': nrm(ks[5], (n_b, d, d), d ** -0.5 * DEEPNORM_BETA),
        'sinks_b': nrm(ks[6], (n_b, N_HEADS), 0.5),
        'w_qkv_c': nrm(ks[7], (n_c, d, 3 * d), d ** -0.5),
        'w_o_c': nrm(ks[8], (n_c, d, d), d ** -0.5 * DEEPNORM_BETA),
        'ln1_g': 1.0 + nrm(ks[9], (DEPTH, d), 0.02),
        'ln1_b': nrm(ks[10], (DEPTH, d), 0.02),
        'ln2_g': 1.0 + nrm(ks[11], (DEPTH, d), 0.02),
        'ln2_b': nrm(ks[12], (DEPTH, d), 0.02),
        'w_grp': nrm(ks[13], (DEPTH, d, N_GROUPS), d ** -0.5),
        'b_grp': nrm(ks[14], (DEPTH, N_GROUPS), 0.01),
        'w_exp': nrm(ks[15], (DEPTH, d, N_EXPERTS), d ** -0.5),
        'b_exp': nrm(ks[16], (DEPTH, N_EXPERTS), 0.01),
        'w_e_gate': nrm(ks[17], (DEPTH, N_EXPERTS, d, D_EXPERT), d ** -0.5),
        'w_e_up': nrm(ks[18], (DEPTH, N_EXPERTS, d, D_EXPERT), d ** -0.5),
        'w_e_down': nrm(ks[19], (DEPTH, N_EXPERTS, D_EXPERT, d), D_EXPERT ** -0.5 * DEEPNORM_BETA),
        'w_ple_gate': nrm(ks[20], (DEPTH, d, d), d ** -0.5),
        'w_ple_proj': nrm(ks[21], (DEPTH, PLE_DIM, d), PLE_DIM ** -0.5),
    }


def reference(x, p, w_qkv_a, w_o_a, w_qkv_b, w_o_b, sinks_b, w_qkv_c, w_o_c,
              ln1_g, ln1_b, ln2_g, ln2_b, w_grp, b_grp, w_exp, b_exp,
              w_e_gate, w_e_up, w_e_down, w_ple_gate, w_ple_proj):
    bsz, seq, d = x.shape
    kv_w = SWA_KV_HEADS * HEAD_DIM
    for i in range(DEPTH):
        mixer, j = i % N_MIXERS, i // N_MIXERS
        if mixer == 0:
            q, k, v = jnp.split(x @ w_qkv_a[j], 3, axis=-1)
            o = moba_attention(q.reshape(bsz, seq, N_HEADS, HEAD_DIM), k.reshape(bsz, seq, N_HEADS, HEAD_DIM),
                               v.reshape(bsz, seq, N_HEADS, HEAD_DIM))
            h = o.reshape(bsz, seq, d) @ w_o_a[j]
        elif mixer == 1:
            qkv = x @ w_qkv_b[j]
            q = qkv[..., :d].reshape(bsz, seq, N_HEADS, HEAD_DIM)
            k = qkv[..., d:d + kv_w].reshape(bsz, seq, SWA_KV_HEADS, HEAD_DIM)
            v = qkv[..., d + kv_w:].reshape(bsz, seq, SWA_KV_HEADS, HEAD_DIM)
            h = swa_gqa_attention(q, k, v, sinks_b[j]).reshape(bsz, seq, d) @ w_o_b[j]
        else:
            q, k, v = jnp.split(x @ w_qkv_c[j], 3, axis=-1)
            o = stick_breaking_attention(q.reshape(bsz, seq, N_HEADS, HEAD_DIM), k.reshape(bsz, seq, N_HEADS, HEAD_DIM),
                                         v.reshape(bsz, seq, N_HEADS, HEAD_DIM))
            h = o.reshape(bsz, seq, d) @ w_o_c[j]
        x = layer_norm(DEEPNORM_ALPHA * x + h, ln1_g[i], ln1_b[i])
        m = hierarchical_moe(x, w_grp[i], b_grp[i], w_exp[i], b_exp[i], w_e_gate[i], w_e_up[i], w_e_down[i])
        x = layer_norm(DEEPNORM_ALPHA * x + m, ln2_g[i], ln2_b[i])
        ple_gate = jax.nn.sigmoid((x @ w_ple_gate[i]).astype(jnp.float32)).astype(x.dtype)
        x = x + ple_gate * (p[i] @ w_ple_proj[i])
    return x
```

```python
import functools

import jax
import jax.numpy as jnp
import numpy as np
from jax import lax
from jax.experimental import pallas as pl
from jax.experimental.pallas import tpu as pltpu

F32 = jnp.float32
BF16 = jnp.bfloat16

LANES = 128
HEAD_DIM = 64
N_HEADS = 16
N_PAIRS = N_HEADS // 2
MOBA_BLOCK = 256
MOBA_TOPK = 3
SWA_WINDOW = 128
SWA_KV_HEADS = 4
SWA_GROUP = N_HEADS // SWA_KV_HEADS
SB_BLOCK = 128
N_GROUPS = 4
EXPERTS_PER_GROUP = 8
N_EXPERTS = N_GROUPS * EXPERTS_PER_GROUP
MOE_BLOCK = 256
LN_EPS = 1e-5
ROUTER_ROWS = 8 + N_EXPERTS
VMEM_LIMIT = 48 * 1024 * 1024

NT_DIMS = (((1,), (1,)), ((), ()))


def _alibi_slopes():
    return jnp.asarray(np.array([2.0 ** (-8.0 * (h + 1) / N_HEADS) for h in range(N_HEADS)], dtype=np.float32))


def _params(*semantics):
    return pltpu.CompilerParams(dimension_semantics=semantics, vmem_limit_bytes=VMEM_LIMIT)


def _head_masks():
    lane = lax.broadcasted_iota(jnp.int32, (1, LANES), 1)
    return (lane < HEAD_DIM, lane >= HEAD_DIM)


def _proj_kernel(x_ref, w_ref, o_ref):
    o_ref[...] = jnp.dot(x_ref[...].astype(BF16), w_ref[...], preferred_element_type=F32).astype(o_ref.dtype)


def _proj(x, w, tm=512, tn=512):
    t, k = x.shape
    n = w.shape[1]
    tm = min(tm, t)
    return pl.pallas_call(
        _proj_kernel,
        grid=(t // tm, n // tn),
        in_specs=[pl.BlockSpec((tm, k), lambda i, j: (i, 0)), pl.BlockSpec((k, tn), lambda i, j: (0, j))],
        out_specs=pl.BlockSpec((tm, tn), lambda i, j: (i, j)),
        out_shape=jax.ShapeDtypeStruct((t, n), BF16),
        compiler_params=_params("parallel", "parallel"),
        name="qkv_proj",
    )(x, w)


def _layer_norm(y, g, b):
    mu = jnp.mean(y, axis=-1, keepdims=True)
    d = y - mu
    var = jnp.mean(d * d, axis=-1, keepdims=True)
    return d * lax.rsqrt(var + LN_EPS) * g + b


def _oproj_ln_kernel(o_ref, w_ref, x_ref, g_ref, b_ref, out_ref, *, alpha):
    h = jnp.dot(o_ref[...], w_ref[...], preferred_element_type=F32)
    out_ref[...] = _layer_norm(alpha * x_ref[...] + h, g_ref[...], b_ref[...])


def _oproj_ln(o, w, x, g, b, alpha, tm=512):
    t, d = x.shape
    tm = min(tm, t)
    row = pl.BlockSpec((tm, d), lambda i: (i, 0))
    vec = pl.BlockSpec((1, d), lambda i: (0, 0))
    return pl.pallas_call(
        functools.partial(_oproj_ln_kernel, alpha=alpha),
        grid=(t // tm,),
        in_specs=[row, pl.BlockSpec((d, d), lambda i: (0, 0)), row, vec, vec],
        out_specs=row,
        out_shape=jax.ShapeDtypeStruct((t, d), F32),
        compiler_params=_params("parallel"),
        name="oproj_ln",
    )(o, w, x, g.reshape(1, d), b.reshape(1, d))


def _ln_ple_kernel(x_ref, m0_ref, m1_ref, w0_ref, w1_ref, g_ref, b_ref, p_ref, wg_ref, wp_ref, out_ref, *, alpha):
    m = m0_ref[...] * w0_ref[:, 0:1] + m1_ref[...] * w1_ref[:, 0:1]
    x2 = _layer_norm(alpha * x_ref[...] + m, g_ref[...], b_ref[...])
    gate = jax.nn.sigmoid(jnp.dot(x2.astype(BF16), wg_ref[...], preferred_element_type=F32))
    proj = jnp.dot(p_ref[...].astype(BF16), wp_ref[...], preferred_element_type=F32)
    out_ref[...] = x2 + gate * proj


def _ln_ple(x, m2, w0, w1, g, b, p, wg, wp, alpha, tm=256):
    t, d = x.shape
    pd = p.shape[1]
    tm = min(tm, t)
    nt = t // tm
    row = pl.BlockSpec((tm, d), lambda i: (i, 0))
    rep = pl.BlockSpec((tm, LANES), lambda i: (i, 0))
    vec = pl.BlockSpec((1, d), lambda i: (0, 0))
    return pl.pallas_call(
        functools.partial(_ln_ple_kernel, alpha=alpha),
        grid=(nt,),
        in_specs=[row, row, pl.BlockSpec((tm, d), lambda i: (i + nt, 0)), rep, rep, vec, vec,
                  pl.BlockSpec((tm, pd), lambda i: (i, 0)),
                  pl.BlockSpec((d, d), lambda i: (0, 0)), pl.BlockSpec((pd, d), lambda i: (0, 0))],
        out_specs=row,
        out_shape=jax.ShapeDtypeStruct((t, d), F32),
        compiler_params=_params("parallel"),
        name="ln_ple",
    )(x, m2, m2, w0, w1, g.reshape(1, d), b.reshape(1, d), p, wg, wp)


def _moba_kernel(slopes_ref, q_ref, k_ref, v_ref, o_ref, kmean_ref, *, n_blk, scale):
    pair = pl.program_id(1)
    qi = pl.program_id(2)
    blk = MOBA_BLOCK

    @pl.when(qi == 0)
    def _():
        for j in range(n_blk):
            kb = k_ref[j * blk:(j + 1) * blk, :].astype(F32)
            kmean_ref[j:j + 1, :] = jnp.sum(kb, axis=0, keepdims=True) * (1.0 / blk)

    q2 = q_ref[...]
    kmean = kmean_ref[...].astype(BF16)
    row = lax.broadcasted_iota(jnp.int32, (blk, blk), 0)
    col = lax.broadcasted_iota(jnp.int32, (blk, blk), 1)
    dpos = (row - col).astype(F32)
    eye = (row == col).astype(BF16)
    jrow = lax.broadcasted_iota(jnp.int32, (n_blk, blk), 0)
    jlane = lax.broadcasted_iota(jnp.int32, (1, n_blk), 1)
    own0 = pl.multiple_of(qi * blk, blk)
    k_own = k_ref[pl.ds(own0, blk), :]
    v_own = v_ref[pl.ds(own0, blk), :]

    out = jnp.zeros((blk, LANES), F32)
    for hh, hmask in enumerate(_head_masks()):
        slope = slopes_ref[2 * pair + hh]
        qh = jnp.where(hmask, q2, jnp.zeros_like(q2))

        gate = lax.dot_general(kmean, qh, NT_DIMS, preferred_element_type=F32)
        past = jrow < qi
        sel = jnp.zeros((n_blk, blk), F32)
        for j in range(n_blk):
            gj = gate[j:j + 1, :]
            beats = past & ((gate > gj) | ((gate == gj) & (jrow < j)))
            n_beat = jnp.sum(beats.astype(F32), axis=0, keepdims=True)
            sel = jnp.where(jrow == j, (n_beat < MOBA_TOPK).astype(F32), sel)
        sel = jnp.where(past, sel, 0.0)
        sel_q = lax.dot_general(eye, sel.astype(BF16), NT_DIMS, preferred_element_type=F32)

        s = lax.dot_general(qh, k_own, NT_DIMS, preferred_element_type=F32) * scale - slope * dpos
        s = jnp.where(dpos >= 0, s, -jnp.inf)
        m = jnp.max(s, axis=1, keepdims=True)
        p = jnp.exp(s - m)
        l = jnp.sum(p, axis=1, keepdims=True)
        acc = jnp.dot(p.astype(BF16), v_own, preferred_element_type=F32)

        def past_block(j, carry, qh=qh, slope=slope, sel_q=sel_q):
            m, l, acc = carry
            j0 = pl.multiple_of(j * blk, blk)
            kb = k_ref[pl.ds(j0, blk), :]
            vb = v_ref[pl.ds(j0, blk), :]
            dist = dpos + ((qi - j) * blk).astype(F32)
            s = lax.dot_general(qh, kb, NT_DIMS, preferred_element_type=F32) * scale - slope * dist
            chosen = jnp.sum(jnp.where(jlane == j, sel_q, 0.0), axis=1, keepdims=True)
            s = jnp.where(chosen > 0.5, s, -jnp.inf)
            m_new = jnp.maximum(m, jnp.max(s, axis=1, keepdims=True))
            a = jnp.exp(m - m_new)
            p = jnp.exp(s - m_new)
            l = a * l + jnp.sum(p, axis=1, keepdims=True)
            acc = a * acc + jnp.dot(p.astype(BF16), vb, preferred_element_type=F32)
            return m_new, l, acc

        m, l, acc = lax.fori_loop(0, qi, past_block, (m, l, acc))
        out = jnp.where(hmask, acc / l, out)
    o_ref[...] = out.astype(o_ref.dtype)


def _moba_attention(qkv, d):
    bsz, seq, _ = qkv.shape
    assert seq % MOBA_BLOCK == 0
    n_blk = seq // MOBA_BLOCK
    ncol = d // LANES
    return pl.pallas_call(
        functools.partial(_moba_kernel, n_blk=n_blk, scale=HEAD_DIM ** -0.5),
        grid=(bsz, N_PAIRS, n_blk),
        in_specs=[pl.BlockSpec(memory_space=pltpu.SMEM),
                  pl.BlockSpec((None, MOBA_BLOCK, LANES), lambda b, p, i: (b, i, p)),
                  pl.BlockSpec((None, seq, LANES), lambda b, p, i: (b, 0, ncol + p)),
                  pl.BlockSpec((None, seq, LANES), lambda b, p, i: (b, 0, 2 * ncol + p))],
        out_specs=pl.BlockSpec((None, MOBA_BLOCK, LANES), lambda b, p, i: (b, i, p)),
        out_shape=jax.ShapeDtypeStruct((bsz, seq, d), BF16),
        scratch_shapes=[pltpu.VMEM((n_blk, LANES), F32)],
        compiler_params=_params("parallel", "parallel", "arbitrary"),
        name="moba_attn",
    )(_alibi_slopes(), qkv, qkv, qkv)


def _swa_head_order():
    order = []
    for pair in range(N_PAIRS):
        r, i = divmod(pair, SWA_GROUP)
        order += [(2 * r) * SWA_GROUP + i, (2 * r + 1) * SWA_GROUP + i]
    return order


def _swa_kernel(slopes_ref, sinks_ref, q_ref, k_ref, v_ref, o_ref, *, scale):
    pair = pl.program_id(1)
    n = pl.program_id(2)
    win = SWA_WINDOW
    r = pair // SWA_GROUP
    i = pair % SWA_GROUP
    q2 = q_ref[...]
    cur0 = pl.multiple_of(n * win, win)
    prev0 = pl.multiple_of(jnp.maximum(n - 1, 0) * win, win)
    k_cur = k_ref[pl.ds(cur0, win), :]
    v_cur = v_ref[pl.ds(cur0, win), :]
    k_prev = k_ref[pl.ds(prev0, win), :]
    v_prev = v_ref[pl.ds(prev0, win), :]
    row = lax.broadcasted_iota(jnp.int32, (win, win), 0)
    col = lax.broadcasted_iota(jnp.int32, (win, win), 1)
    d_cur = (row - col).astype(F32)
    d_prev = d_cur + float(win)
    ok_cur = row >= col
    ok_prev = (col > row) & (n > 0)

    out = jnp.zeros((win, LANES), F32)
    for hh, hmask in enumerate(_head_masks()):
        head = (2 * r + hh) * SWA_GROUP + i
        slope = slopes_ref[head]
        sink = sinks_ref[head]
        qh = jnp.where(hmask, q2, jnp.zeros_like(q2))
        s_cur = lax.dot_general(qh, k_cur, NT_DIMS, preferred_element_type=F32) * scale - slope * d_cur
        s_cur = jnp.where(ok_cur, s_cur, -jnp.inf)
        s_prev = lax.dot_general(qh, k_prev, NT_DIMS, preferred_element_type=F32) * scale - slope * d_prev
        s_prev = jnp.where(ok_prev, s_prev, -jnp.inf)
        m = jnp.maximum(jnp.max(s_cur, axis=1, keepdims=True), jnp.max(s_prev, axis=1, keepdims=True))
        m = jnp.maximum(m, sink)
        p_cur = jnp.exp(s_cur - m)
        p_prev = jnp.exp(s_prev - m)
        l = jnp.sum(p_cur, axis=1, keepdims=True) + jnp.sum(p_prev, axis=1, keepdims=True) + jnp.exp(sink - m)
        o = (jnp.dot(p_cur.astype(BF16), v_cur, preferred_element_type=F32)
             + jnp.dot(p_prev.astype(BF16), v_prev, preferred_element_type=F32))
        out = jnp.where(hmask, o / l, out)
    o_ref[...] = out.astype(o_ref.dtype)


def _swa_attention(qkv, sinks, d):
    bsz, seq, _ = qkv.shape
    assert seq % SWA_WINDOW == 0
    ncol = d // LANES
    kv_col = SWA_KV_HEADS * HEAD_DIM // LANES
    return pl.pallas_call(
        functools.partial(_swa_kernel, scale=HEAD_DIM ** -0.5),
        grid=(bsz, N_PAIRS, seq // SWA_WINDOW),
        in_specs=[pl.BlockSpec(memory_space=pltpu.SMEM), pl.BlockSpec(memory_space=pltpu.SMEM),
                  pl.BlockSpec((None, SWA_WINDOW, LANES), lambda b, p, i: (b, i, p)),
                  pl.BlockSpec((None, seq, LANES), lambda b, p, i: (b, 0, ncol + p // SWA_GROUP)),
                  pl.BlockSpec((None, seq, LANES), lambda b, p, i: (b, 0, ncol + kv_col + p // SWA_GROUP))],
        out_specs=pl.BlockSpec((None, SWA_WINDOW, LANES), lambda b, p, i: (b, i, p)),
        out_shape=jax.ShapeDtypeStruct((bsz, seq, d), BF16),
        compiler_params=_params("parallel", "parallel", "arbitrary"),
        name="swa_attn",
    )(_alibi_slopes(), sinks.astype(F32), qkv, qkv, qkv)


def _log_sigmoid(z):
    return jnp.minimum(z, 0.0) - jnp.log(1.0 + jnp.exp(-jnp.abs(z)))


def _suffix_sums(lk, upper):
    hi = lk.astype(BF16)
    lo = (lk - hi.astype(F32)).astype(BF16)
    return (jnp.dot(hi, upper, preferred_element_type=F32) + jnp.dot(lo, upper, preferred_element_type=F32))


def _sb_kernel(q_ref, k_ref, v_ref, o_ref, *, scale):
    c = pl.program_id(2)
    blk = SB_BLOCK
    q2 = q_ref[...]
    row = lax.broadcasted_iota(jnp.int32, (blk, blk), 0)
    col = lax.broadcasted_iota(jnp.int32, (blk, blk), 1)
    strict = col < row
    upper = (row > col).astype(BF16)
    own0 = pl.multiple_of(c * blk, blk)
    k_own = k_ref[pl.ds(own0, blk), :]
    v_own = v_ref[pl.ds(own0, blk), :]

    out = jnp.zeros((blk, LANES), F32)
    for hmask in _head_masks():
        qh = jnp.where(hmask, q2, jnp.zeros_like(q2))
        z = lax.dot_general(qh, k_own, NT_DIMS, preferred_element_type=F32) * scale
        ls = _log_sigmoid(z)
        lk = jnp.where(strict, ls - z, 0.0)
        a = jnp.where(strict, jnp.exp(ls + _suffix_sums(lk, upper)), 0.0)
        acc = jnp.dot(a.astype(BF16), v_own, preferred_element_type=F32)
        later = jnp.sum(lk, axis=1, keepdims=True)

        def earlier_block(step, carry, qh=qh):
            later, acc = carry
            j0 = pl.multiple_of((c - 1 - step) * blk, blk)
            kb = k_ref[pl.ds(j0, blk), :]
            vb = v_ref[pl.ds(j0, blk), :]
            z = lax.dot_general(qh, kb, NT_DIMS, preferred_element_type=F32) * scale
            ls = _log_sigmoid(z)
            lk = ls - z
            a = jnp.exp(ls + _suffix_sums(lk, upper) + later)
            acc = acc + jnp.dot(a.astype(BF16), vb, preferred_element_type=F32)
            return later + jnp.sum(lk, axis=1, keepdims=True), acc

        _, acc = lax.fori_loop(0, c, earlier_block, (later, acc))
        out = jnp.where(hmask, acc, out)
    o_ref[...] = out.astype(o_ref.dtype)


def _sb_attention(qkv, d):
    bsz, seq, _ = qkv.shape
    assert seq % SB_BLOCK == 0
    ncol = d // LANES
    return pl.pallas_call(
        functools.partial(_sb_kernel, scale=HEAD_DIM ** -0.5),
        grid=(bsz, N_PAIRS, seq // SB_BLOCK),
        in_specs=[pl.BlockSpec((None, SB_BLOCK, LANES), lambda b, p, i: (b, i, p)),
                  pl.BlockSpec((None, seq, LANES), lambda b, p, i: (b, 0, ncol + p)),
                  pl.BlockSpec((None, seq, LANES), lambda b, p, i: (b, 0, 2 * ncol + p))],
        out_specs=pl.BlockSpec((None, SB_BLOCK, LANES), lambda b, p, i: (b, i, p)),
        out_shape=jax.ShapeDtypeStruct((bsz, seq, d), BF16),
        compiler_params=_params("parallel", "parallel", "arbitrary"),
        name="sb_attn",
    )(qkv, qkv, qkv)


def _split_bf16(a):
    hi = a.astype(BF16)
    return hi, (a - hi.astype(F32)).astype(BF16)


def _router_kernel(x_ref, w_ref, b_ref, ids_ref, gates_ref):
    xh, xl = _split_bf16(x_ref[...])
    wh, wl = _split_bf16(w_ref[...])
    logits = (lax.dot_general(wh, xh, NT_DIMS, preferred_element_type=F32)
              + lax.dot_general(wh, xl, NT_DIMS, preferred_element_type=F32)
              + lax.dot_general(wl, xh, NT_DIMS, preferred_element_type=F32)) + b_ref[...]
    tm = logits.shape[1]
    gl = logits[0:N_GROUPS, :]
    g_row = lax.broadcasted_iota(jnp.int32, (N_GROUPS, tm), 0).astype(F32)
    g_max = jnp.max(gl, axis=0, keepdims=True)
    g_p = 1.0 / jnp.sum(jnp.exp(gl - g_max), axis=0, keepdims=True)
    g_idx = jnp.min(jnp.where(gl == g_max, g_row, float(N_GROUPS)), axis=0, keepdims=True)

    in_grp = jnp.zeros((EXPERTS_PER_GROUP, tm), F32)
    for g in range(N_GROUPS):
        lo = 8 + g * EXPERTS_PER_GROUP
        in_grp = jnp.where(g_idx == float(g), logits[lo:lo + EXPERTS_PER_GROUP, :], in_grp)
    e_exp = jnp.exp(in_grp - jnp.max(in_grp, axis=0, keepdims=True))
    e_p = e_exp / jnp.sum(e_exp, axis=0, keepdims=True)
    e_row = lax.broadcasted_iota(jnp.int32, (EXPERTS_PER_GROUP, tm), 0).astype(F32)
    p1 = jnp.max(e_p, axis=0, keepdims=True)
    i1 = jnp.min(jnp.where(e_p == p1, e_row, float(EXPERTS_PER_GROUP)), axis=0, keepdims=True)
    rest = jnp.where(e_row == i1, -1.0, e_p)
    p2 = jnp.max(rest, axis=0, keepdims=True)
    i2 = jnp.min(jnp.where(rest == p2, e_row, float(EXPERTS_PER_GROUP)), axis=0, keepdims=True)
    den = p1 + p2
    base = g_idx * float(EXPERTS_PER_GROUP)
    ids_ref[0:1, :] = (base + i1).astype(jnp.int32)
    ids_ref[1:2, :] = (base + i2).astype(jnp.int32)
    gates_ref[0:1, :] = g_p * (p1 / den)
    gates_ref[1:2, :] = g_p * (p2 / den)


def _router(x, w_rt, b_rt, tm=512):
    t, d = x.shape
    tm = min(tm, t)
    return pl.pallas_call(
        _router_kernel,
        grid=(t // tm,),
        in_specs=[pl.BlockSpec((tm, d), lambda i: (i, 0)),
                  pl.BlockSpec((ROUTER_ROWS, d), lambda i: (0, 0)),
                  pl.BlockSpec((ROUTER_ROWS, 1), lambda i: (0, 0))],
        out_specs=[pl.BlockSpec((2, tm), lambda i: (0, i)), pl.BlockSpec((2, tm), lambda i: (0, i))],
        out_shape=[jax.ShapeDtypeStruct((2, t), jnp.int32), jax.ShapeDtypeStruct((2, t), F32)],
        compiler_params=_params("parallel"),
        name="moe_router",
    )(x, w_rt, b_rt)


def _row_copy(src_hbm, dst_hbm, sem, src_row, dst_row):
    return pltpu.make_async_copy(src_hbm.at[pl.ds(src_row, 1)], dst_hbm.at[pl.ds(dst_row, 1)], sem)


def _gather_rows_kernel(nused_ref, idx_hbm, src_hbm, dst_hbm, idx_smem, idx_sem, row_sem):
    i = pl.program_id(0)

    @pl.when(i < nused_ref[0])
    def _():
        idx_copy = pltpu.make_async_copy(idx_hbm.at[i], idx_smem, idx_sem)
        idx_copy.start()
        idx_copy.wait()
        base = i * MOE_BLOCK

        def issue(r, _):
            _row_copy(src_hbm, dst_hbm, row_sem, idx_smem[r], base + r).start()
            return 0

        def drain(r, _):
            _row_copy(src_hbm, dst_hbm, row_sem, 0, base + r).wait()
            return 0

        lax.fori_loop(0, MOE_BLOCK, issue, 0)
        lax.fori_loop(0, MOE_BLOCK, drain, 0)


def _gather_rows(nused, idx, src, n_rows):
    n_blocks = idx.shape[0]
    return pl.pallas_call(
        _gather_rows_kernel,
        grid_spec=pltpu.PrefetchScalarGridSpec(
            num_scalar_prefetch=1,
            grid=(n_blocks,),
            in_specs=[pl.BlockSpec(memory_space=pl.ANY), pl.BlockSpec(memory_space=pl.ANY)],
            out_specs=pl.BlockSpec(memory_space=pl.ANY),
            scratch_shapes=[pltpu.SMEM((MOE_BLOCK,), jnp.int32), pltpu.SemaphoreType.DMA, pltpu.SemaphoreType.DMA]),
        out_shape=jax.ShapeDtypeStruct((n_rows, src.shape[1]), src.dtype),
        compiler_params=_params("arbitrary"),
        name="moe_gather",
    )(nused, idx, src)


def _scatter_rows_kernel(nvalid_ref, idx_hbm, src_hbm, dst_hbm, idx_smem, idx_sem, row_sem):
    i = pl.program_id(0)
    nv = nvalid_ref[i]

    @pl.when(nv > 0)
    def _():
        idx_copy = pltpu.make_async_copy(idx_hbm.at[i], idx_smem, idx_sem)
        idx_copy.start()
        idx_copy.wait()
        base = i * MOE_BLOCK

        def issue(r, _):
            _row_copy(src_hbm, dst_hbm, row_sem, base + r, idx_smem[r]).start()
            return 0

        def drain(r, _):
            _row_copy(src_hbm, dst_hbm, row_sem, base + r, 0).wait()
            return 0

        lax.fori_loop(0, nv, issue, 0)
        lax.fori_loop(0, nv, drain, 0)


def _scatter_rows(nvalid, idx, src, n_rows):
    n_blocks = idx.shape[0]
    return pl.pallas_call(
        _scatter_rows_kernel,
        grid_spec=pltpu.PrefetchScalarGridSpec(
            num_scalar_prefetch=1,
            grid=(n_blocks,),
            in_specs=[pl.BlockSpec(memory_space=pl.ANY), pl.BlockSpec(memory_space=pl.ANY)],
            out_specs=pl.BlockSpec(memory_space=pl.ANY),
            scratch_shapes=[pltpu.SMEM((MOE_BLOCK,), jnp.int32), pltpu.SemaphoreType.DMA, pltpu.SemaphoreType.DMA]),
        out_shape=jax.ShapeDtypeStruct((n_rows, src.shape[1]), src.dtype),
        compiler_params=_params("arbitrary"),
        name="moe_scatter",
    )(nvalid, idx, src)


def _expert_kernel(block_e_ref, nused_ref, x_ref, wg_ref, wu_ref, wd_ref, y_ref):
    @pl.when(pl.program_id(0) < nused_ref[0])
    def _():
        xb = x_ref[...].astype(BF16)
        g = jnp.dot(xb, wg_ref[...], preferred_element_type=F32)
        u = jnp.dot(xb, wu_ref[...], preferred_element_type=F32)
        h = (g * jax.nn.sigmoid(g)) * u
        y_ref[...] = jnp.dot(h.astype(BF16), wd_ref[...], preferred_element_type=F32)


def _expert_ffn(block_e, nused, xs, wg, wu, wd):
    n_rows, d = xs.shape
    de = wg.shape[2]
    n_blocks = n_rows // MOE_BLOCK

    def rows(i, be, nu):
        return (jnp.minimum(i, nu[0] - 1), 0)

    return pl.pallas_call(
        _expert_kernel,
        grid_spec=pltpu.PrefetchScalarGridSpec(
            num_scalar_prefetch=2,
            grid=(n_blocks,),
            in_specs=[pl.BlockSpec((MOE_BLOCK, d), rows),
                      pl.BlockSpec((None, d, de), lambda i, be, nu: (be[i], 0, 0)),
                      pl.BlockSpec((None, d, de), lambda i, be, nu: (be[i], 0, 0)),
                      pl.BlockSpec((None, de, d), lambda i, be, nu: (be[i], 0, 0))],
            out_specs=pl.BlockSpec((MOE_BLOCK, d), rows)),
        out_shape=jax.ShapeDtypeStruct((n_rows, d), F32),
        compiler_params=_params("arbitrary"),
        name="moe_experts",
    )(block_e, nused, xs, wg, wu, wd)


def _dispatch_plan(ids, n_tok):
    n_asg = 2 * n_tok
    n_blocks = n_asg // MOE_BLOCK + N_EXPERTS
    flat_e = ids.reshape(-1)
    order = jnp.argsort(flat_e, stable=True).astype(jnp.int32)
    counts = jnp.sum((flat_e[:, None] == jnp.arange(N_EXPERTS, dtype=jnp.int32)[None, :]).astype(jnp.int32), axis=0)
    start = jnp.cumsum(counts) - counts
    nblk = (counts + MOE_BLOCK - 1) // MOE_BLOCK
    bend = jnp.cumsum(nblk)
    bstart = bend - nblk
    nused = bend[-1:].astype(jnp.int32)
    bl = jnp.arange(n_blocks, dtype=jnp.int32)
    block_e = jnp.minimum(jnp.searchsorted(bend, bl, side="right"), N_EXPERTS - 1).astype(jnp.int32)
    j0 = (bl - bstart[block_e]) * MOE_BLOCK
    nvalid = jnp.where(bl < nused[0], jnp.clip(counts[block_e] - j0, 0, MOE_BLOCK), 0).astype(jnp.int32)
    r = jnp.arange(MOE_BLOCK, dtype=jnp.int32)[None, :]
    pos = jnp.clip(start[block_e][:, None] + j0[:, None] + r, 0, n_asg - 1)
    row_asg = jnp.where(r < nvalid[:, None], order[pos], 0).astype(jnp.int32)
    row_tok = row_asg % n_tok
    return block_e, nused, nvalid, row_tok, row_asg


def _moe_outputs(x, w_rt, b_rt, wg, wu, wd):
    n_tok, d = x.shape
    ids, gates = _router(x, w_rt, b_rt)
    block_e, nused, nvalid, row_tok, row_asg = _dispatch_plan(ids, n_tok)
    n_rows = row_tok.shape[0] * MOE_BLOCK
    xs = _gather_rows(nused, row_tok, x, n_rows)
    ys = _expert_ffn(block_e, nused, xs, wg, wu, wd)
    return _scatter_rows(nvalid, row_asg, ys, 2 * n_tok), gates


def _router_weights(w_grp, b_grp, w_exp, b_exp):
    d = w_grp.shape[0]
    pad = 8 - N_GROUPS
    w_rt = jnp.concatenate([w_grp.T, jnp.zeros((pad, d), F32), w_exp.T], axis=0)
    b_rt = jnp.concatenate([b_grp, jnp.zeros((pad,), F32), b_exp]).reshape(ROUTER_ROWS, 1)
    return w_rt.astype(F32), b_rt.astype(F32)


def kernel(x, p, w_qkv_a, w_o_a, w_qkv_b, w_o_b, sinks_b, w_qkv_c, w_o_c, ln1_g, ln1_b, ln2_g, ln2_b, w_grp, b_grp, w_exp, b_exp, w_e_gate, w_e_up, w_e_down, w_ple_gate, w_ple_proj):
    bsz, seq, d = x.shape
    depth = p.shape[0]
    n_tok = bsz * seq
    alpha = (2.0 * depth) ** 0.25
    kv_w = SWA_KV_HEADS * HEAD_DIM
    swa_cols = jnp.asarray(np.concatenate([np.arange(h * HEAD_DIM, (h + 1) * HEAD_DIM) for h in _swa_head_order()]))

    xt = x.reshape(n_tok, d)
    for i in range(depth):
        mixer, j = i % 3, i // 3
        if mixer == 0:
            qkv = _proj(xt, w_qkv_a[j].astype(BF16))
            o = _moba_attention(qkv.reshape(bsz, seq, 3 * d), d)
            w_o = w_o_a[j]
        elif mixer == 1:
            w_qkv = jnp.concatenate([w_qkv_b[j][:, :d][:, swa_cols], w_qkv_b[j][:, d:]], axis=1)
            qkv = _proj(xt, w_qkv.astype(BF16))
            o = _swa_attention(qkv.reshape(bsz, seq, d + 2 * kv_w), sinks_b[j], d)
            w_o = w_o_b[j][swa_cols, :]
        else:
            qkv = _proj(xt, w_qkv_c[j].astype(BF16))
            o = _sb_attention(qkv.reshape(bsz, seq, 3 * d), d)
            w_o = w_o_c[j]
        x1 = _oproj_ln(o.reshape(n_tok, d), w_o.astype(BF16), xt, ln1_g[i], ln1_b[i], alpha)

        w_rt, b_rt = _router_weights(w_grp[i], b_grp[i], w_exp[i], b_exp[i])
        m2, gates = _moe_outputs(x1, w_rt, b_rt, w_e_gate[i].astype(BF16), w_e_up[i].astype(BF16),
                                 w_e_down[i].astype(BF16))
        w0 = jnp.broadcast_to(gates[0][:, None], (n_tok, LANES))
        w1 = jnp.broadcast_to(gates[1][:, None], (n_tok, LANES))
        xt = _ln_ple(x1, m2, w0, w1, ln2_g[i], ln2_b[i], p[i].reshape(n_tok, -1),
                     w_ple_gate[i].astype(BF16), w_ple_proj[i].astype(BF16), alpha)
    return xt.reshape(bsz, seq, d)
```

```python
import functools

import jax
import jax.numpy as jnp
import numpy as np
from jax import lax
from jax.experimental import pallas as pl
from jax.experimental.pallas import tpu as pltpu
from jax.experimental.pallas import tpu_sc as plsc

F32 = jnp.float32
BF16 = jnp.bfloat16

LANES = 128
HEAD_DIM = 64
N_HEADS = 16
N_PAIRS = N_HEADS // 2
MOBA_BLOCK = 256
MOBA_TOPK = 3
SWA_WINDOW = 128
SWA_KV_HEADS = 4
SWA_GROUP = N_HEADS // SWA_KV_HEADS
SB_TILE = 512
N_GROUPS = 4
EXPERTS_PER_GROUP = 8
N_EXPERTS = N_GROUPS * EXPERTS_PER_GROUP
MOE_BLOCK = 256
LN_EPS = 1e-5
ROUTER_ROWS = 8 + N_EXPERTS
N_SLABS = 4
SC_WINDOW = 128
VMEM_LIMIT = 48 * 1024 * 1024

NT_DIMS = (((1,), (1,)), ((), ()))


def _alibi_slopes():
    return jnp.asarray(np.array([2.0 ** (-8.0 * (h + 1) / N_HEADS) for h in range(N_HEADS)], dtype=np.float32))


def _params(*semantics):
    return pltpu.CompilerParams(dimension_semantics=semantics, vmem_limit_bytes=VMEM_LIMIT)


def _head_masks():
    lane = lax.broadcasted_iota(jnp.int32, (1, LANES), 1)
    return (lane < HEAD_DIM, lane >= HEAD_DIM)


def _proj_kernel(x_ref, w_ref, o_ref):
    o_ref[...] = jnp.dot(x_ref[...].astype(BF16), w_ref[...], preferred_element_type=F32).astype(o_ref.dtype)


def _proj(x, w, tm=512, tn=512):
    t, k = x.shape
    n = w.shape[1]
    tm = min(tm, t)
    return pl.pallas_call(
        _proj_kernel,
        grid=(t // tm, n // tn),
        in_specs=[pl.BlockSpec((tm, k), lambda i, j: (i, 0)), pl.BlockSpec((k, tn), lambda i, j: (0, j))],
        out_specs=pl.BlockSpec((tm, tn), lambda i, j: (i, j)),
        out_shape=jax.ShapeDtypeStruct((t, n), BF16),
        compiler_params=_params("parallel", "parallel"),
        name="qkv_proj",
    )(x, w)


def _layer_norm(y, g, b):
    mu = jnp.mean(y, axis=-1, keepdims=True)
    d = y - mu
    var = jnp.mean(d * d, axis=-1, keepdims=True)
    return d * lax.rsqrt(var + LN_EPS) * g + b


def _to_slabs(slab_ref, y):
    w = y.shape[1] // N_SLABS
    for c in range(N_SLABS):
        slab_ref[c] = y[:, c * w:(c + 1) * w]


def _from_slabs(slab_ref):
    return jnp.concatenate([slab_ref[c] for c in range(N_SLABS)], axis=1)


def _oproj_ln_kernel(o_ref, w_ref, x_ref, g_ref, b_ref, out_ref, slab_ref, *, alpha):
    h = jnp.dot(o_ref[...], w_ref[...], preferred_element_type=F32)
    y = _layer_norm(alpha * x_ref[...] + h, g_ref[...], b_ref[...])
    out_ref[...] = y
    _to_slabs(slab_ref, y)


def _oproj_ln(o, w, x, g, b, alpha, tm=512):
    t, d = x.shape
    tm = min(tm, t)
    row = pl.BlockSpec((tm, d), lambda i: (i, 0))
    vec = pl.BlockSpec((1, d), lambda i: (0, 0))
    return pl.pallas_call(
        functools.partial(_oproj_ln_kernel, alpha=alpha),
        grid=(t // tm,),
        in_specs=[row, pl.BlockSpec((d, d), lambda i: (0, 0)), row, vec, vec],
        out_specs=[row, pl.BlockSpec((N_SLABS, tm, d // N_SLABS), lambda i: (0, i, 0))],
        out_shape=[jax.ShapeDtypeStruct((t, d), F32), jax.ShapeDtypeStruct((N_SLABS, t, d // N_SLABS), F32)],
        compiler_params=_params("parallel"),
        name="oproj_ln",
    )(o, w, x, g.reshape(1, d), b.reshape(1, d))


def _ln_ple_kernel(x_ref, m0_ref, m1_ref, w0_ref, w1_ref, g_ref, b_ref, p_ref, wg_ref, wp_ref, out_ref, *, alpha):
    m = _from_slabs(m0_ref) * w0_ref[:, 0:1] + _from_slabs(m1_ref) * w1_ref[:, 0:1]
    x2 = _layer_norm(alpha * x_ref[...] + m, g_ref[...], b_ref[...])
    gate = jax.nn.sigmoid(jnp.dot(x2.astype(BF16), wg_ref[...], preferred_element_type=F32))
    proj = jnp.dot(p_ref[...].astype(BF16), wp_ref[...], preferred_element_type=F32)
    out_ref[...] = x2 + gate * proj


def _ln_ple(x, m2, w0, w1, g, b, p, wg, wp, alpha, tm=256):
    t, d = x.shape
    pd = p.shape[1]
    tm = min(tm, t)
    nt = t // tm
    row = pl.BlockSpec((tm, d), lambda i: (i, 0))
    rep = pl.BlockSpec((tm, LANES), lambda i: (i, 0))
    vec = pl.BlockSpec((1, d), lambda i: (0, 0))
    return pl.pallas_call(
        functools.partial(_ln_ple_kernel, alpha=alpha),
        grid=(nt,),
        in_specs=[row, pl.BlockSpec((N_SLABS, tm, d // N_SLABS), lambda i: (0, i, 0)),
                  pl.BlockSpec((N_SLABS, tm, d // N_SLABS), lambda i: (0, i + nt, 0)), rep, rep, vec, vec,
                  pl.BlockSpec((tm, pd), lambda i: (i, 0)),
                  pl.BlockSpec((d, d), lambda i: (0, 0)), pl.BlockSpec((pd, d), lambda i: (0, 0))],
        out_specs=row,
        out_shape=jax.ShapeDtypeStruct((t, d), F32),
        compiler_params=_params("parallel"),
        name="ln_ple",
    )(x, m2, m2, w0, w1, g.reshape(1, d), b.reshape(1, d), p, wg, wp)


def _moba_kernel(slopes_ref, q_ref, k_ref, v_ref, o_ref, kmean_ref, vt_ref, bias_ref, *, n_blk, scale):
    pair = pl.program_id(1)
    qi = pl.program_id(2)
    blk = MOBA_BLOCK

    @pl.when(qi == 0)
    def _():
        for j in range(n_blk):
            kb = k_ref[j * blk:(j + 1) * blk, :].astype(F32)
            kmean_ref[j:j + 1, :] = jnp.sum(kb, axis=0, keepdims=True) * (1.0 / blk)
            vt_ref[j] = v_ref[j * blk:(j + 1) * blk, :].astype(F32).T.astype(BF16)

    kmean = kmean_ref[...].astype(BF16)
    krow = lax.broadcasted_iota(jnp.int32, (blk, blk), 0)
    qcol = lax.broadcasted_iota(jnp.int32, (blk, blk), 1)
    dpos = (qcol - krow).astype(F32)
    jrow = lax.broadcasted_iota(jnp.int32, (n_blk, blk), 0)
    past = jrow < qi
    own0 = pl.multiple_of(qi * blk, blk)
    k_own = k_ref[pl.ds(own0, blk), :]
    vt_own = vt_ref[qi]
    q2 = q_ref[...]

    heads, carry = [], []
    for hh, hmask in enumerate(_head_masks()):
        slope = slopes_ref[2 * pair + hh]
        qh = jnp.where(hmask, q2, jnp.zeros_like(q2))

        gate = lax.dot_general(kmean, qh, NT_DIMS, preferred_element_type=F32)
        sel = jnp.zeros((n_blk, blk), F32)
        for j in range(n_blk):
            gj = gate[j:j + 1, :]
            beats = past & ((gate > gj) | ((gate == gj) & (jrow < j)))
            n_beat = jnp.sum(beats.astype(F32), axis=0, keepdims=True)
            sel = jnp.where(jrow == j, (n_beat < MOBA_TOPK).astype(F32), sel)
        bias_ref[hh] = jnp.where(past & (sel > 0.5), 0.0, -jnp.inf)

        qs = qh * jnp.asarray(scale, BF16)
        sd = slope * dpos
        s = lax.dot_general(k_own, qs, NT_DIMS, preferred_element_type=F32) - sd
        s = jnp.where(dpos >= 0, s, -jnp.inf)
        m = jnp.max(s, axis=0, keepdims=True)
        p = jnp.exp(s - m)
        l = jnp.sum(p, axis=0, keepdims=True)
        acc = jnp.dot(vt_own, p.astype(BF16), preferred_element_type=F32)
        heads.append((qs, sd, slope))
        carry += [m, l, acc]

    def past_block(j, carry):
        kb = k_ref[pl.ds(pl.multiple_of(j * blk, blk), blk), :]
        vtb = vt_ref[j]
        new = []
        for hh, (qs, sd, slope) in enumerate(heads):
            m, l, acc = carry[3 * hh:3 * hh + 3]
            rb = bias_ref[hh, pl.ds(j, 1), :] - slope * ((qi - j) * blk).astype(F32)
            s = (lax.dot_general(kb, qs, NT_DIMS, preferred_element_type=F32) - sd) + rb
            m_new = jnp.maximum(m, jnp.max(s, axis=0, keepdims=True))
            a = jnp.exp(m - m_new)
            p = jnp.exp(s - m_new)
            l = a * l + jnp.sum(p, axis=0, keepdims=True)
            acc = a * acc + jnp.dot(vtb, p.astype(BF16), preferred_element_type=F32)
            new += [m_new, l, acc]
        return tuple(new)

    carry = lax.fori_loop(0, qi, past_block, tuple(carry))
    out_t = jnp.concatenate([(carry[2] / carry[1])[:HEAD_DIM], (carry[5] / carry[4])[HEAD_DIM:]], axis=0)
    o_ref[...] = out_t.T.astype(o_ref.dtype)


def _moba_attention(qkv, d):
    bsz, seq, _ = qkv.shape
    assert seq % MOBA_BLOCK == 0
    n_blk = seq // MOBA_BLOCK
    ncol = d // LANES
    return pl.pallas_call(
        functools.partial(_moba_kernel, n_blk=n_blk, scale=HEAD_DIM ** -0.5),
        grid=(bsz, N_PAIRS, n_blk),
        in_specs=[pl.BlockSpec(memory_space=pltpu.SMEM),
                  pl.BlockSpec((None, MOBA_BLOCK, LANES), lambda b, p, i: (b, i, p)),
                  pl.BlockSpec((None, seq, LANES), lambda b, p, i: (b, 0, ncol + p)),
                  pl.BlockSpec((None, seq, LANES), lambda b, p, i: (b, 0, 2 * ncol + p))],
        out_specs=pl.BlockSpec((None, MOBA_BLOCK, LANES), lambda b, p, i: (b, i, p)),
        out_shape=jax.ShapeDtypeStruct((bsz, seq, d), BF16),
        scratch_shapes=[pltpu.VMEM((n_blk, LANES), F32), pltpu.VMEM((n_blk, LANES, MOBA_BLOCK), BF16),
                        pltpu.VMEM((2, n_blk, MOBA_BLOCK), F32)],
        compiler_params=_params("parallel", "parallel", "arbitrary"),
        name="moba_attn",
    )(_alibi_slopes(), qkv, qkv, qkv)


def _swa_head_order():
    order = []
    for pair in range(N_PAIRS):
        r, i = divmod(pair, SWA_GROUP)
        order += [(2 * r) * SWA_GROUP + i, (2 * r + 1) * SWA_GROUP + i]
    return order


def _swa_kernel(slopes_ref, sinks_ref, q_ref, k_ref, v_ref, o_ref, *, scale):
    pair = pl.program_id(1)
    n = pl.program_id(2)
    win = SWA_WINDOW
    r = pair // SWA_GROUP
    i = pair % SWA_GROUP
    q2 = q_ref[...]
    cur0 = pl.multiple_of(n * win, win)
    prev0 = pl.multiple_of(jnp.maximum(n - 1, 0) * win, win)
    k_cur = k_ref[pl.ds(cur0, win), :]
    v_cur = v_ref[pl.ds(cur0, win), :]
    k_prev = k_ref[pl.ds(prev0, win), :]
    v_prev = v_ref[pl.ds(prev0, win), :]
    row = lax.broadcasted_iota(jnp.int32, (win, win), 0)
    col = lax.broadcasted_iota(jnp.int32, (win, win), 1)
    d_cur = (row - col).astype(F32)
    d_prev = d_cur + float(win)
    ok_cur = row >= col
    ok_prev = (col > row) & (n > 0)

    out = jnp.zeros((win, LANES), F32)
    for hh, hmask in enumerate(_head_masks()):
        head = (2 * r + hh) * SWA_GROUP + i
        slope = slopes_ref[head]
        sink = sinks_ref[head]
        qh = jnp.where(hmask, q2, jnp.zeros_like(q2))
        s_cur = lax.dot_general(qh, k_cur, NT_DIMS, preferred_element_type=F32) * scale - slope * d_cur
        s_cur = jnp.where(ok_cur, s_cur, -jnp.inf)
        s_prev = lax.dot_general(qh, k_prev, NT_DIMS, preferred_element_type=F32) * scale - slope * d_prev
        s_prev = jnp.where(ok_prev, s_prev, -jnp.inf)
        m = jnp.maximum(jnp.max(s_cur, axis=1, keepdims=True), jnp.max(s_prev, axis=1, keepdims=True))
        m = jnp.maximum(m, sink)
        p_cur = jnp.exp(s_cur - m)
        p_prev = jnp.exp(s_prev - m)
        l = jnp.sum(p_cur, axis=1, keepdims=True) + jnp.sum(p_prev, axis=1, keepdims=True) + jnp.exp(sink - m)
        o = (jnp.dot(p_cur.astype(BF16), v_cur, preferred_element_type=F32)
             + jnp.dot(p_prev.astype(BF16), v_prev, preferred_element_type=F32))
        out = jnp.where(hmask, o / l, out)
    o_ref[...] = out.astype(o_ref.dtype)


def _swa_attention(qkv, sinks, d):
    bsz, seq, _ = qkv.shape
    assert seq % SWA_WINDOW == 0
    ncol = d // LANES
    kv_col = SWA_KV_HEADS * HEAD_DIM // LANES
    return pl.pallas_call(
        functools.partial(_swa_kernel, scale=HEAD_DIM ** -0.5),
        grid=(bsz, N_PAIRS, seq // SWA_WINDOW),
        in_specs=[pl.BlockSpec(memory_space=pltpu.SMEM), pl.BlockSpec(memory_space=pltpu.SMEM),
                  pl.BlockSpec((None, SWA_WINDOW, LANES), lambda b, p, i: (b, i, p)),
                  pl.BlockSpec((None, seq, LANES), lambda b, p, i: (b, 0, ncol + p // SWA_GROUP)),
                  pl.BlockSpec((None, seq, LANES), lambda b, p, i: (b, 0, ncol + kv_col + p // SWA_GROUP))],
        out_specs=pl.BlockSpec((None, SWA_WINDOW, LANES), lambda b, p, i: (b, i, p)),
        out_shape=jax.ShapeDtypeStruct((bsz, seq, d), BF16),
        compiler_params=_params("parallel", "parallel", "arbitrary"),
        name="swa_attn",
    )(_alibi_slopes(), sinks.astype(F32), qkv, qkv, qkv)


def _log_sigmoid(z):
    return jnp.minimum(z, 0.0) - jnp.log(1.0 + jnp.exp(-jnp.abs(z)))


def _suffix_sums(lk, upper):
    hi = lk.astype(BF16)
    lo = (lk - hi.astype(F32)).astype(BF16)
    n = lk.shape[0]
    both = jnp.dot(jnp.concatenate([hi, lo], axis=0), upper, preferred_element_type=F32)
    return both[:n] + both[n:]


def _sb_tile(qh, kb, vb, upper, later, strict, scale):
    z = lax.dot_general(qh, kb, NT_DIMS, preferred_element_type=F32) * scale
    ls = _log_sigmoid(z)
    lk = ls - z
    if strict is not None:
        lk = jnp.where(strict, lk, 0.0)
    after = _suffix_sums(lk, upper)
    if later is not None:
        after = after + later
    a = jnp.exp(ls + after)
    if strict is not None:
        a = jnp.where(strict, a, 0.0)
    mass = jnp.sum(lk, axis=1, keepdims=True)
    return jnp.dot(a.astype(BF16), vb, preferred_element_type=F32), (mass if later is None else later + mass)


def _sb_kernel(q_ref, k_ref, v_ref, o_ref, *, scale):
    c = pl.program_id(2)
    blk = SB_TILE
    row = lax.broadcasted_iota(jnp.int32, (blk, blk), 0)
    col = lax.broadcasted_iota(jnp.int32, (blk, blk), 1)
    strict = col < row
    upper = (row > col).astype(BF16)
    own0 = pl.multiple_of(c * blk, blk)
    k_own = k_ref[pl.ds(own0, blk), :]
    v_own = v_ref[pl.ds(own0, blk), :]
    q2 = q_ref[...]
    masks = _head_masks()
    qhs = [jnp.where(hmask, q2, jnp.zeros_like(q2)) for hmask in masks]

    carry = []
    for qh in qhs:
        acc, later = _sb_tile(qh, k_own, v_own, upper, None, strict, scale)
        carry += [later, acc]

    def earlier_block(step, carry):
        j0 = pl.multiple_of((c - 1 - step) * blk, blk)
        kb = k_ref[pl.ds(j0, blk), :]
        vb = v_ref[pl.ds(j0, blk), :]
        new = []
        for hh, qh in enumerate(qhs):
            contrib, later = _sb_tile(qh, kb, vb, upper, carry[2 * hh], None, scale)
            new += [later, carry[2 * hh + 1] + contrib]
        return tuple(new)

    carry = lax.fori_loop(0, c, earlier_block, tuple(carry))
    o_ref[...] = jnp.where(masks[0], carry[1], carry[3]).astype(o_ref.dtype)


def _sb_attention(qkv, d):
    bsz, seq, _ = qkv.shape
    blk = min(SB_TILE, seq)
    assert seq % blk == 0 and blk == SB_TILE
    ncol = d // LANES
    return pl.pallas_call(
        functools.partial(_sb_kernel, scale=HEAD_DIM ** -0.5),
        grid=(bsz, N_PAIRS, seq // blk),
        in_specs=[pl.BlockSpec((None, blk, LANES), lambda b, p, i: (b, i, p)),
                  pl.BlockSpec((None, seq, LANES), lambda b, p, i: (b, 0, ncol + p)),
                  pl.BlockSpec((None, seq, LANES), lambda b, p, i: (b, 0, 2 * ncol + p))],
        out_specs=pl.BlockSpec((None, blk, LANES), lambda b, p, i: (b, i, p)),
        out_shape=jax.ShapeDtypeStruct((bsz, seq, d), BF16),
        compiler_params=_params("parallel", "parallel", "arbitrary"),
        name="sb_attn",
    )(qkv, qkv, qkv)


def _split_bf16(a):
    hi = a.astype(BF16)
    return hi, (a - hi.astype(F32)).astype(BF16)


def _router_kernel(x_ref, w_ref, b_ref, ids_ref, gates_ref):
    xh, xl = _split_bf16(x_ref[...])
    wh, wl = _split_bf16(w_ref[...])
    logits = (lax.dot_general(wh, xh, NT_DIMS, preferred_element_type=F32)
              + lax.dot_general(wh, xl, NT_DIMS, preferred_element_type=F32)
              + lax.dot_general(wl, xh, NT_DIMS, preferred_element_type=F32)) + b_ref[...]
    tm = logits.shape[1]
    gl = logits[0:N_GROUPS, :]
    g_row = lax.broadcasted_iota(jnp.int32, (N_GROUPS, tm), 0).astype(F32)
    g_max = jnp.max(gl, axis=0, keepdims=True)
    g_p = 1.0 / jnp.sum(jnp.exp(gl - g_max), axis=0, keepdims=True)
    g_idx = jnp.min(jnp.where(gl == g_max, g_row, float(N_GROUPS)), axis=0, keepdims=True)

    in_grp = jnp.zeros((EXPERTS_PER_GROUP, tm), F32)
    for g in range(N_GROUPS):
        lo = 8 + g * EXPERTS_PER_GROUP
        in_grp = jnp.where(g_idx == float(g), logits[lo:lo + EXPERTS_PER_GROUP, :], in_grp)
    e_exp = jnp.exp(in_grp - jnp.max(in_grp, axis=0, keepdims=True))
    e_p = e_exp / jnp.sum(e_exp, axis=0, keepdims=True)
    e_row = lax.broadcasted_iota(jnp.int32, (EXPERTS_PER_GROUP, tm), 0).astype(F32)
    p1 = jnp.max(e_p, axis=0, keepdims=True)
    i1 = jnp.min(jnp.where(e_p == p1, e_row, float(EXPERTS_PER_GROUP)), axis=0, keepdims=True)
    rest = jnp.where(e_row == i1, -1.0, e_p)
    p2 = jnp.max(rest, axis=0, keepdims=True)
    i2 = jnp.min(jnp.where(rest == p2, e_row, float(EXPERTS_PER_GROUP)), axis=0, keepdims=True)
    den = p1 + p2
    base = g_idx * float(EXPERTS_PER_GROUP)
    ids_ref[0:1, :] = (base + i1).astype(jnp.int32)
    ids_ref[1:2, :] = (base + i2).astype(jnp.int32)
    gates_ref[0:1, :] = g_p * (p1 / den)
    gates_ref[1:2, :] = g_p * (p2 / den)


def _router(x, w_rt, b_rt, tm=512):
    t, d = x.shape
    tm = min(tm, t)
    return pl.pallas_call(
        _router_kernel,
        grid=(t // tm,),
        in_specs=[pl.BlockSpec((tm, d), lambda i: (i, 0)),
                  pl.BlockSpec((ROUTER_ROWS, d), lambda i: (0, 0)),
                  pl.BlockSpec((ROUTER_ROWS, 1), lambda i: (0, 0))],
        out_specs=[pl.BlockSpec((2, tm), lambda i: (0, i)), pl.BlockSpec((2, tm), lambda i: (0, i))],
        out_shape=[jax.ShapeDtypeStruct((2, t), jnp.int32), jax.ShapeDtypeStruct((2, t), F32)],
        compiler_params=_params("parallel"),
        name="moe_router",
    )(x, w_rt, b_rt)


def _sc_gather_rows(table, idx):
    n = idx.shape[0]
    width = table.shape[1]
    mesh = plsc.VectorSubcoreMesh(core_axis_name="core", subcore_axis_name="subcore")

    @pl.kernel(out_type=jax.ShapeDtypeStruct((n, width), table.dtype), mesh=mesh, scratch_types=[])
    def gather(table_hbm, idx_hbm, out_hbm):
        def step(idx_vmem, out_vmem):
            pltpu.sync_copy(table_hbm.at[idx_vmem.at[0]], out_vmem)

        pltpu.emit_pipeline(
            step,
            grid=(n // SC_WINDOW,),
            in_specs=[pl.BlockSpec((1, SC_WINDOW), index_map=lambda i: (0, i))],
            out_specs=[pl.BlockSpec((SC_WINDOW, width), index_map=lambda i: (i, 0))],
            core_axis_name=("core", "subcore"),
            dimension_semantics=(pltpu.PARALLEL,),
        )(idx_hbm, out_hbm)

    return gather(table, idx.reshape(1, n))


def _gather_slabs(slabs, rows):
    n_slab, n_rows, width = slabs.shape
    idx = (jnp.arange(n_slab, dtype=jnp.int32)[:, None] * n_rows + rows[None, :]).reshape(-1)
    return _sc_gather_rows(slabs.reshape(n_slab * n_rows, width), idx).reshape(n_slab, rows.shape[0], width)


def _expert_kernel(block_e_ref, nused_ref, x_ref, wg_ref, wu_ref, wd_ref, y_ref):
    @pl.when(pl.program_id(0) < nused_ref[0])
    def _():
        xb = _from_slabs(x_ref).astype(BF16)
        g = jnp.dot(xb, wg_ref[...], preferred_element_type=F32)
        u = jnp.dot(xb, wu_ref[...], preferred_element_type=F32)
        h = (g * jax.nn.sigmoid(g)) * u
        _to_slabs(y_ref, jnp.dot(h.astype(BF16), wd_ref[...], preferred_element_type=F32))


def _expert_ffn(block_e, nused, xs, wg, wu, wd):
    n_slab, n_rows, width = xs.shape
    d, de = wg.shape[1], wg.shape[2]
    n_blocks = n_rows // MOE_BLOCK

    def rows(i, be, nu):
        return (0, jnp.minimum(i, nu[0] - 1), 0)

    return pl.pallas_call(
        _expert_kernel,
        grid_spec=pltpu.PrefetchScalarGridSpec(
            num_scalar_prefetch=2,
            grid=(n_blocks,),
            in_specs=[pl.BlockSpec((n_slab, MOE_BLOCK, width), rows),
                      pl.BlockSpec((None, d, de), lambda i, be, nu: (be[i], 0, 0)),
                      pl.BlockSpec((None, d, de), lambda i, be, nu: (be[i], 0, 0)),
                      pl.BlockSpec((None, de, d), lambda i, be, nu: (be[i], 0, 0))],
            out_specs=pl.BlockSpec((n_slab, MOE_BLOCK, width), rows)),
        out_shape=jax.ShapeDtypeStruct((n_slab, n_rows, width), F32),
        compiler_params=_params("arbitrary"),
        name="moe_experts",
    )(block_e, nused, xs, wg, wu, wd)


def _dispatch_plan(ids, n_tok):
    n_asg = 2 * n_tok
    n_blocks = n_asg // MOE_BLOCK + N_EXPERTS
    flat_e = ids.reshape(-1)
    order = jnp.argsort(flat_e, stable=True).astype(jnp.int32)
    rank = jnp.argsort(order).astype(jnp.int32)
    counts = jnp.sum((flat_e[:, None] == jnp.arange(N_EXPERTS, dtype=jnp.int32)[None, :]).astype(jnp.int32), axis=0)
    start = jnp.cumsum(counts) - counts
    nblk = (counts + MOE_BLOCK - 1) // MOE_BLOCK
    bend = jnp.cumsum(nblk)
    bstart = bend - nblk
    nused = bend[-1:].astype(jnp.int32)
    bl = jnp.arange(n_blocks, dtype=jnp.int32)
    block_e = jnp.minimum(jnp.searchsorted(bend, bl, side="right"), N_EXPERTS - 1).astype(jnp.int32)
    j0 = (bl - bstart[block_e]) * MOE_BLOCK
    nvalid = jnp.where(bl < nused[0], jnp.clip(counts[block_e] - j0, 0, MOE_BLOCK), 0)
    r = jnp.arange(MOE_BLOCK, dtype=jnp.int32)[None, :]
    pos = jnp.clip(start[block_e][:, None] + j0[:, None] + r, 0, n_asg - 1)
    row_tok = (jnp.where(r < nvalid[:, None], order[pos], 0) % n_tok).astype(jnp.int32).reshape(-1)
    asg_row = (bstart[flat_e] * MOE_BLOCK + rank - start[flat_e]).astype(jnp.int32)
    return block_e, nused, row_tok, asg_row


def _moe_outputs(x, x_slabs, w_rt, b_rt, wg, wu, wd):
    n_tok = x.shape[0]
    ids, gates = _router(x, w_rt, b_rt)
    block_e, nused, row_tok, asg_row = _dispatch_plan(ids, n_tok)
    xs = _gather_slabs(x_slabs, row_tok)
    ys = _expert_ffn(block_e, nused, xs, wg, wu, wd)
    return _gather_slabs(ys, asg_row), gates


def _router_weights(w_grp, b_grp, w_exp, b_exp):
    d = w_grp.shape[0]
    pad = 8 - N_GROUPS
    w_rt = jnp.concatenate([w_grp.T, jnp.zeros((pad, d), F32), w_exp.T], axis=0)
    b_rt = jnp.concatenate([b_grp, jnp.zeros((pad,), F32), b_exp]).reshape(ROUTER_ROWS, 1)
    return w_rt.astype(F32), b_rt.astype(F32)


def kernel(x, p, w_qkv_a, w_o_a, w_qkv_b, w_o_b, sinks_b, w_qkv_c, w_o_c, ln1_g, ln1_b, ln2_g, ln2_b, w_grp, b_grp, w_exp, b_exp, w_e_gate, w_e_up, w_e_down, w_ple_gate, w_ple_proj):
    bsz, seq, d = x.shape
    depth = p.shape[0]
    n_tok = bsz * seq
    alpha = (2.0 * depth) ** 0.25
    kv_w = SWA_KV_HEADS * HEAD_DIM
    swa_cols = jnp.asarray(np.concatenate([np.arange(h * HEAD_DIM, (h + 1) * HEAD_DIM) for h in _swa_head_order()]))

    xt = x.reshape(n_tok, d)
    for i in range(depth):
        mixer, j = i % 3, i // 3
        if mixer == 0:
            qkv = _proj(xt, w_qkv_a[j].astype(BF16))
            o = _moba_attention(qkv.reshape(bsz, seq, 3 * d), d)
            w_o = w_o_a[j]
        elif mixer == 1:
            w_qkv = jnp.concatenate([w_qkv_b[j][:, :d][:, swa_cols], w_qkv_b[j][:, d:]], axis=1)
            qkv = _proj(xt, w_qkv.astype(BF16))
            o = _swa_attention(qkv.reshape(bsz, seq, d + 2 * kv_w), sinks_b[j], d)
            w_o = w_o_b[j][swa_cols, :]
        else:
            qkv = _proj(xt, w_qkv_c[j].astype(BF16))
            o = _sb_attention(qkv.reshape(bsz, seq, 3 * d), d)
            w_o = w_o_c[j]
        x1, x1_slabs = _oproj_ln(o.reshape(n_tok, d), w_o.astype(BF16), xt, ln1_g[i], ln1_b[i], alpha)

        w_rt, b_rt = _router_weights(w_grp[i], b_grp[i], w_exp[i], b_exp[i])
        m2, gates = _moe_outputs(x1, x1_slabs, w_rt, b_rt, w_e_gate[i].astype(BF16), w_e_up[i].astype(BF16),
                                 w_e_down[i].astype(BF16))
        w0 = jnp.broadcast_to(gates[0][:, None], (n_tok, LANES))
        w1 = jnp.broadcast_to(gates[1][:, None], (n_tok, LANES))
        xt = _ln_ple(x1, m2, w0, w1, ln2_g[i], ln2_b[i], p[i].reshape(n_tok, -1),
                     w_ple_gate[i].astype(BF16), w_ple_proj[i].astype(BF16), alpha)
    return xt.reshape(bsz, seq, d)
```

```python
import functools

import jax
import jax.numpy as jnp
import numpy as np
from jax import lax
from jax.experimental import pallas as pl
from jax.experimental.pallas import tpu as pltpu
from jax.experimental.pallas import tpu_sc as plsc

F32 = jnp.float32
BF16 = jnp.bfloat16

LANES = 128
HEAD_DIM = 64
N_HEADS = 16
N_PAIRS = N_HEADS // 2
MOBA_BLOCK = 256
MOBA_TOPK = 3
SWA_WINDOW = 128
SWA_KV_HEADS = 4
SWA_GROUP = N_HEADS // SWA_KV_HEADS
SWA_QWINS = 4
SB_TILE = 512
N_GROUPS = 4
EXPERTS_PER_GROUP = 8
N_EXPERTS = N_GROUPS * EXPERTS_PER_GROUP
MOE_BLOCK = 512
LN_EPS = 1e-5
ROUTER_ROWS = 8 + N_EXPERTS
N_SLABS = 4
SC_WINDOW = 128
VMEM_LIMIT = 48 * 1024 * 1024

NT_DIMS = (((1,), (1,)), ((), ()))


def _alibi_slopes():
    return jnp.asarray(np.array([2.0 ** (-8.0 * (h + 1) / N_HEADS) for h in range(N_HEADS)], dtype=np.float32))


def _params(*semantics):
    return pltpu.CompilerParams(dimension_semantics=semantics, vmem_limit_bytes=VMEM_LIMIT)


def _head_masks():
    lane = lax.broadcasted_iota(jnp.int32, (1, LANES), 1)
    return (lane < HEAD_DIM, lane >= HEAD_DIM)


def _proj_kernel(x_ref, w_ref, o_ref):
    o_ref[...] = jnp.dot(x_ref[...].astype(BF16), w_ref[...], preferred_element_type=F32).astype(o_ref.dtype)


def _proj(x, w, tm=1024, tn=512):
    t, k = x.shape
    n = w.shape[1]
    tm = min(tm, t)
    return pl.pallas_call(
        _proj_kernel,
        grid=(t // tm, n // tn),
        in_specs=[pl.BlockSpec((tm, k), lambda i, j: (i, 0)), pl.BlockSpec((k, tn), lambda i, j: (0, j))],
        out_specs=pl.BlockSpec((tm, tn), lambda i, j: (i, j)),
        out_shape=jax.ShapeDtypeStruct((t, n), BF16),
        compiler_params=_params("parallel", "parallel"),
        name="qkv_proj",
    )(x, w)


def _layer_norm(y, g, b):
    mu = jnp.mean(y, axis=-1, keepdims=True)
    d = y - mu
    var = jnp.mean(d * d, axis=-1, keepdims=True)
    return d * lax.rsqrt(var + LN_EPS) * g + b


def _to_slabs(slab_ref, y):
    w = y.shape[1] // N_SLABS
    for c in range(N_SLABS):
        slab_ref[c] = y[:, c * w:(c + 1) * w]


def _from_slabs(slab_ref):
    return jnp.concatenate([slab_ref[c] for c in range(N_SLABS)], axis=1)


def _oproj_ln_kernel(o_ref, w_ref, x_ref, g_ref, b_ref, out_ref, slab_ref, *, alpha):
    h = jnp.dot(o_ref[...], w_ref[...], preferred_element_type=F32)
    y = _layer_norm(alpha * x_ref[...] + h, g_ref[...], b_ref[...])
    out_ref[...] = y
    _to_slabs(slab_ref, y)


def _oproj_ln(o, w, x, g, b, alpha, tm=512):
    t, d = x.shape
    tm = min(tm, t)
    row = pl.BlockSpec((tm, d), lambda i: (i, 0))
    vec = pl.BlockSpec((1, d), lambda i: (0, 0))
    return pl.pallas_call(
        functools.partial(_oproj_ln_kernel, alpha=alpha),
        grid=(t // tm,),
        in_specs=[row, pl.BlockSpec((d, d), lambda i: (0, 0)), row, vec, vec],
        out_specs=[row, pl.BlockSpec((N_SLABS, tm, d // N_SLABS), lambda i: (0, i, 0))],
        out_shape=[jax.ShapeDtypeStruct((t, d), F32), jax.ShapeDtypeStruct((N_SLABS, t, d // N_SLABS), F32)],
        compiler_params=_params("parallel"),
        name="oproj_ln",
    )(o, w, x, g.reshape(1, d), b.reshape(1, d))


def _ln_ple_kernel(x_ref, m0_ref, m1_ref, w0_ref, w1_ref, g_ref, b_ref, p_ref, wg_ref, wp_ref, out_ref, *, alpha):
    m = _from_slabs(m0_ref) * w0_ref[:, 0:1] + _from_slabs(m1_ref) * w1_ref[:, 0:1]
    x2 = _layer_norm(alpha * x_ref[...] + m, g_ref[...], b_ref[...])
    gate = jax.nn.sigmoid(jnp.dot(x2.astype(BF16), wg_ref[...], preferred_element_type=F32))
    proj = jnp.dot(p_ref[...].astype(BF16), wp_ref[...], preferred_element_type=F32)
    out_ref[...] = x2 + gate * proj


def _ln_ple(x, m2, w0, w1, g, b, p, layer, wg, wp, alpha, tm=512):
    t, d = x.shape
    pd = p.shape[1]
    tm = min(tm, t)
    nt = t // tm
    p0 = layer * nt
    row = pl.BlockSpec((tm, d), lambda i: (i, 0))
    rep = pl.BlockSpec((tm, LANES), lambda i: (i, 0))
    vec = pl.BlockSpec((1, d), lambda i: (0, 0))
    return pl.pallas_call(
        functools.partial(_ln_ple_kernel, alpha=alpha),
        grid=(nt,),
        in_specs=[row, pl.BlockSpec((N_SLABS, tm, d // N_SLABS), lambda i: (0, i, 0)),
                  pl.BlockSpec((N_SLABS, tm, d // N_SLABS), lambda i: (0, i + nt, 0)), rep, rep, vec, vec,
                  pl.BlockSpec((tm, pd), lambda i: (p0 + i, 0)),
                  pl.BlockSpec((d, d), lambda i: (0, 0)), pl.BlockSpec((pd, d), lambda i: (0, 0))],
        out_specs=row,
        out_shape=jax.ShapeDtypeStruct((t, d), F32),
        compiler_params=_params("parallel"),
        name="ln_ple",
    )(x, m2, m2, w0, w1, g.reshape(1, d), b.reshape(1, d), p, wg, wp)


def _moba_kernel(slopes_ref, q_ref, k_ref, v_ref, o_ref, kmean_ref, vt_ref, bias_ref, *, n_blk, scale):
    pair = pl.program_id(1)
    qi = pl.program_id(2)
    blk = MOBA_BLOCK

    @pl.when(qi == 0)
    def _():
        for j in range(n_blk):
            kb = k_ref[j * blk:(j + 1) * blk, :].astype(F32)
            kmean_ref[j:j + 1, :] = jnp.sum(kb, axis=0, keepdims=True) * (1.0 / blk)
            vt_ref[j] = v_ref[j * blk:(j + 1) * blk, :].astype(F32).T.astype(BF16)

    kmean = kmean_ref[...].astype(BF16)
    krow = lax.broadcasted_iota(jnp.int32, (blk, blk), 0)
    qcol = lax.broadcasted_iota(jnp.int32, (blk, blk), 1)
    dpos = (qcol - krow).astype(F32)
    jrow = lax.broadcasted_iota(jnp.int32, (n_blk, blk), 0)
    past = jrow < qi
    q2 = q_ref[...]

    heads = []
    for hh, hmask in enumerate(_head_masks()):
        slope = slopes_ref[2 * pair + hh]
        qh = jnp.where(hmask, q2, jnp.zeros_like(q2))

        gate = lax.dot_general(kmean, qh, NT_DIMS, preferred_element_type=F32)
        sel = jnp.zeros((n_blk, blk), F32)
        for j in range(n_blk):
            gj = gate[j:j + 1, :]
            beats = past & ((gate > gj) | ((gate == gj) & (jrow < j)))
            n_beat = jnp.sum(beats.astype(F32), axis=0, keepdims=True)
            sel = jnp.where(jrow == j, (n_beat < MOBA_TOPK).astype(F32), sel)
        bias_ref[hh] = jnp.where(past & (sel > 0.5), 0.0, -jnp.inf)
        qs = qh * jnp.asarray(scale, BF16)
        heads.append((qs, slope * dpos, slope))

    def scores(j):
        kb = k_ref[pl.ds(pl.multiple_of(j * blk, blk), blk), :]
        return [lax.dot_general(kb, qs, NT_DIMS, preferred_element_type=F32) - sd for qs, sd, _ in heads]

    def absorb(s, j, hh, m, l, acc):
        vt = vt_ref[j][hh * HEAD_DIM:(hh + 1) * HEAD_DIM]
        m_new = jnp.maximum(m, jnp.max(s, axis=0, keepdims=True))
        shift = jnp.where(m_new == -jnp.inf, 0.0, m_new)
        a = jnp.exp(m - shift)
        p = jnp.exp(s - shift)
        l = a * l + jnp.sum(p, axis=0, keepdims=True)
        acc = a * acc + jnp.dot(vt, p.astype(BF16), preferred_element_type=F32)
        return m_new, l, acc

    def past_block(j, carry):
        new = list(scores(j + 1))
        for hh, (_, _, slope) in enumerate(heads):
            m, l, acc = carry[2 + 3 * hh:5 + 3 * hh]
            rb = bias_ref[hh, pl.ds(j, 1), :] - slope * ((qi - j) * blk).astype(F32)
            new += list(absorb(carry[hh] + rb, j, hh, m, l, acc))
        return tuple(new)

    init = list(scores(0))
    for _ in heads:
        init += [jnp.full((1, blk), -jnp.inf, F32), jnp.zeros((1, blk), F32), jnp.zeros((HEAD_DIM, blk), F32)]
    carry = lax.fori_loop(0, qi, past_block, tuple(init))
    outs = []
    for hh in range(len(heads)):
        m, l, acc = carry[2 + 3 * hh:5 + 3 * hh]
        m, l, acc = absorb(jnp.where(dpos >= 0, carry[hh], -jnp.inf), qi, hh, m, l, acc)
        outs.append(acc / l)
    o_ref[...] = jnp.concatenate(outs, axis=0).T.astype(o_ref.dtype)


def _moba_attention(qkv, d):
    bsz, seq, _ = qkv.shape
    assert seq % MOBA_BLOCK == 0
    n_blk = seq // MOBA_BLOCK
    ncol = d // LANES
    return pl.pallas_call(
        functools.partial(_moba_kernel, n_blk=n_blk, scale=HEAD_DIM ** -0.5),
        grid=(bsz, N_PAIRS, n_blk),
        in_specs=[pl.BlockSpec(memory_space=pltpu.SMEM),
                  pl.BlockSpec((None, MOBA_BLOCK, LANES), lambda b, p, i: (b, i, p)),
                  pl.BlockSpec((None, seq, LANES), lambda b, p, i: (b, 0, ncol + p)),
                  pl.BlockSpec((None, seq, LANES), lambda b, p, i: (b, 0, 2 * ncol + p))],
        out_specs=pl.BlockSpec((None, MOBA_BLOCK, LANES), lambda b, p, i: (b, i, p)),
        out_shape=jax.ShapeDtypeStruct((bsz, seq, d), BF16),
        scratch_shapes=[pltpu.VMEM((n_blk, LANES), F32), pltpu.VMEM((n_blk, LANES, MOBA_BLOCK), BF16),
                        pltpu.VMEM((2, n_blk, MOBA_BLOCK), F32)],
        compiler_params=_params("parallel", "parallel", "arbitrary"),
        name="moba_attn",
    )(_alibi_slopes(), qkv, qkv, qkv)


def _swa_head_order():
    order = []
    for pair in range(N_PAIRS):
        r, i = divmod(pair, SWA_GROUP)
        order += [(2 * r) * SWA_GROUP + i, (2 * r + 1) * SWA_GROUP + i]
    return order


def _swa_kernel(slopes_ref, sinks_ref, q_ref, k_ref, v_ref, o_ref, vt_ref, *, scale, n_win):
    pair = pl.program_id(1)
    n = pl.program_id(2)
    win = SWA_WINDOW
    r = pair // SWA_GROUP
    i = pair % SWA_GROUP

    @pl.when(n == 0)
    def _():
        for j in range(n_win):
            vt_ref[j] = v_ref[j * win:(j + 1) * win, :].astype(F32).T.astype(BF16)

    w0 = n * SWA_QWINS
    q0 = pl.multiple_of(w0 * win, win)
    before0 = pl.multiple_of(jnp.maximum(w0 - 1, 0) * win, win)
    k_blocks = [k_ref[pl.ds(before0, win), :]] + [k_ref[pl.ds(q0 + w * win, win), :] for w in range(SWA_QWINS)]
    vt_blocks = [vt_ref[jnp.maximum(w0 - 1, 0)]] + [vt_ref[w0 + w] for w in range(SWA_QWINS)]
    krow = lax.broadcasted_iota(jnp.int32, (win, win), 0)
    qcol = lax.broadcasted_iota(jnp.int32, (win, win), 1)
    d_cur = (qcol - krow).astype(F32)
    d_prev = d_cur + float(win)
    ok_cur = qcol >= krow
    in_window = krow > qcol

    masks = _head_masks()
    per_head = []
    for hh, hmask in enumerate(masks):
        head = (2 * r + hh) * SWA_GROUP + i
        per_head.append((hmask, slopes_ref[head], sinks_ref[head]))

    for w in range(SWA_QWINS):
        q2 = q_ref[w * win:(w + 1) * win, :]
        ok_prev = in_window if w > 0 else in_window & (n > 0)
        outs = []
        for hmask, slope, sink in per_head:
            qs = jnp.where(hmask, q2, jnp.zeros_like(q2)) * jnp.asarray(scale, BF16)
            s_cur = lax.dot_general(k_blocks[w + 1], qs, NT_DIMS, preferred_element_type=F32) - slope * d_cur
            s_cur = jnp.where(ok_cur, s_cur, -jnp.inf)
            s_prev = lax.dot_general(k_blocks[w], qs, NT_DIMS, preferred_element_type=F32) - slope * d_prev
            s_prev = jnp.where(ok_prev, s_prev, -jnp.inf)
            m = jnp.maximum(jnp.max(s_cur, axis=0, keepdims=True), jnp.max(s_prev, axis=0, keepdims=True))
            m = jnp.maximum(m, sink)
            p_cur = jnp.exp(s_cur - m)
            p_prev = jnp.exp(s_prev - m)
            l = jnp.sum(p_cur, axis=0, keepdims=True) + jnp.sum(p_prev, axis=0, keepdims=True) + jnp.exp(sink - m)
            o_t = (jnp.dot(vt_blocks[w + 1], p_cur.astype(BF16), preferred_element_type=F32)
                   + jnp.dot(vt_blocks[w], p_prev.astype(BF16), preferred_element_type=F32))
            outs.append(o_t / l)
        out_t = jnp.concatenate([outs[0][:HEAD_DIM], outs[1][HEAD_DIM:]], axis=0)
        o_ref[w * win:(w + 1) * win, :] = out_t.T.astype(o_ref.dtype)


def _swa_attention(qkv, sinks, d):
    bsz, seq, _ = qkv.shape
    qb = SWA_QWINS * SWA_WINDOW
    assert seq % qb == 0
    n_win = seq // SWA_WINDOW
    ncol = d // LANES
    kv_col = SWA_KV_HEADS * HEAD_DIM // LANES
    return pl.pallas_call(
        functools.partial(_swa_kernel, scale=HEAD_DIM ** -0.5, n_win=n_win),
        grid=(bsz, N_PAIRS, seq // qb),
        in_specs=[pl.BlockSpec(memory_space=pltpu.SMEM), pl.BlockSpec(memory_space=pltpu.SMEM),
                  pl.BlockSpec((None, qb, LANES), lambda b, p, i: (b, i, p)),
                  pl.BlockSpec((None, seq, LANES), lambda b, p, i: (b, 0, ncol + p // SWA_GROUP)),
                  pl.BlockSpec((None, seq, LANES), lambda b, p, i: (b, 0, ncol + kv_col + p // SWA_GROUP))],
        out_specs=pl.BlockSpec((None, qb, LANES), lambda b, p, i: (b, i, p)),
        out_shape=jax.ShapeDtypeStruct((bsz, seq, d), BF16),
        scratch_shapes=[pltpu.VMEM((n_win, LANES, SWA_WINDOW), BF16)],
        compiler_params=_params("parallel", "parallel", "arbitrary"),
        name="swa_attn",
    )(_alibi_slopes(), sinks.astype(F32), qkv, qkv, qkv)


def _log_sigmoid(z):
    return jnp.minimum(z, 0.0) - jnp.log(1.0 + jnp.exp(-jnp.abs(z)))


def _suffix_sums(lk, upper):
    hi = lk.astype(BF16)
    lo = (lk - hi.astype(F32)).astype(BF16)
    n = lk.shape[0]
    both = jnp.dot(jnp.concatenate([hi, lo], axis=0), upper, preferred_element_type=F32)
    return both[:n] + both[n:]


def _sb_tile(qh, kb, vb, upper, later, strict, scale):
    z = lax.dot_general(qh, kb, NT_DIMS, preferred_element_type=F32) * scale
    ls = _log_sigmoid(z)
    lk = ls - z
    if strict is not None:
        lk = jnp.where(strict, lk, 0.0)
    after = _suffix_sums(lk, upper)
    if later is not None:
        after = after + later
    a = jnp.exp(ls + after)
    if strict is not None:
        a = jnp.where(strict, a, 0.0)
    mass = jnp.sum(lk, axis=1, keepdims=True)
    return jnp.dot(a.astype(BF16), vb, preferred_element_type=F32), (mass if later is None else later + mass)


def _sb_kernel(q_ref, k_ref, v_ref, o_ref, *, scale):
    c = pl.program_id(2)
    blk = SB_TILE
    row = lax.broadcasted_iota(jnp.int32, (blk, blk), 0)
    col = lax.broadcasted_iota(jnp.int32, (blk, blk), 1)
    strict = col < row
    upper = (row > col).astype(BF16)
    own0 = pl.multiple_of(c * blk, blk)
    k_own = k_ref[pl.ds(own0, blk), :]
    v_own = v_ref[pl.ds(own0, blk), :]
    q2 = q_ref[...]
    masks = _head_masks()
    qhs = [jnp.where(hmask, q2, jnp.zeros_like(q2)) for hmask in masks]

    carry = []
    for qh in qhs:
        acc, later = _sb_tile(qh, k_own, v_own, upper, None, strict, scale)
        carry += [later, acc]

    def earlier_block(step, carry):
        j0 = pl.multiple_of((c - 1 - step) * blk, blk)
        kb = k_ref[pl.ds(j0, blk), :]
        vb = v_ref[pl.ds(j0, blk), :]
        new = []
        for hh, qh in enumerate(qhs):
            contrib, later = _sb_tile(qh, kb, vb, upper, carry[2 * hh], None, scale)
            new += [later, carry[2 * hh + 1] + contrib]
        return tuple(new)

    carry = lax.fori_loop(0, c, earlier_block, tuple(carry))
    o_ref[...] = jnp.where(masks[0], carry[1], carry[3]).astype(o_ref.dtype)


def _sb_attention(qkv, d):
    bsz, seq, _ = qkv.shape
    blk = min(SB_TILE, seq)
    assert seq % blk == 0 and blk == SB_TILE
    ncol = d // LANES
    return pl.pallas_call(
        functools.partial(_sb_kernel, scale=HEAD_DIM ** -0.5),
        grid=(bsz, N_PAIRS, seq // blk),
        in_specs=[pl.BlockSpec((None, blk, LANES), lambda b, p, i: (b, i, p)),
                  pl.BlockSpec((None, seq, LANES), lambda b, p, i: (b, 0, ncol + p)),
                  pl.BlockSpec((None, seq, LANES), lambda b, p, i: (b, 0, 2 * ncol + p))],
        out_specs=pl.BlockSpec((None, blk, LANES), lambda b, p, i: (b, i, p)),
        out_shape=jax.ShapeDtypeStruct((bsz, seq, d), BF16),
        compiler_params=_params("parallel", "parallel", "arbitrary"),
        name="sb_attn",
    )(qkv, qkv, qkv)


def _split_bf16(a):
    hi = a.astype(BF16)
    return hi, (a - hi.astype(F32)).astype(BF16)


def _router_kernel(x_ref, w_ref, b_ref, ids_ref, gates_ref):
    xh, xl = _split_bf16(x_ref[...])
    wh, wl = _split_bf16(w_ref[...])
    logits = (lax.dot_general(wh, xh, NT_DIMS, preferred_element_type=F32)
              + lax.dot_general(wh, xl, NT_DIMS, preferred_element_type=F32)
              + lax.dot_general(wl, xh, NT_DIMS, preferred_element_type=F32)) + b_ref[...]
    tm = logits.shape[1]
    gl = logits[0:N_GROUPS, :]
    g_row = lax.broadcasted_iota(jnp.int32, (N_GROUPS, tm), 0).astype(F32)
    g_max = jnp.max(gl, axis=0, keepdims=True)
    g_p = 1.0 / jnp.sum(jnp.exp(gl - g_max), axis=0, keepdims=True)
    g_idx = jnp.min(jnp.where(gl == g_max, g_row, float(N_GROUPS)), axis=0, keepdims=True)

    in_grp = jnp.zeros((EXPERTS_PER_GROUP, tm), F32)
    for g in range(N_GROUPS):
        lo = 8 + g * EXPERTS_PER_GROUP
        in_grp = jnp.where(g_idx == float(g), logits[lo:lo + EXPERTS_PER_GROUP, :], in_grp)
    e_exp = jnp.exp(in_grp - jnp.max(in_grp, axis=0, keepdims=True))
    e_p = e_exp / jnp.sum(e_exp, axis=0, keepdims=True)
    e_row = lax.broadcasted_iota(jnp.int32, (EXPERTS_PER_GROUP, tm), 0).astype(F32)
    p1 = jnp.max(e_p, axis=0, keepdims=True)
    i1 = jnp.min(jnp.where(e_p == p1, e_row, float(EXPERTS_PER_GROUP)), axis=0, keepdims=True)
    rest = jnp.where(e_row == i1, -1.0, e_p)
    p2 = jnp.max(rest, axis=0, keepdims=True)
    i2 = jnp.min(jnp.where(rest == p2, e_row, float(EXPERTS_PER_GROUP)), axis=0, keepdims=True)
    den = p1 + p2
    base = g_idx * float(EXPERTS_PER_GROUP)
    ids_ref[0:1, :] = (base + i1).astype(jnp.int32)
    ids_ref[1:2, :] = (base + i2).astype(jnp.int32)
    gates_ref[0:1, :] = g_p * (p1 / den)
    gates_ref[1:2, :] = g_p * (p2 / den)


def _router(x, w_rt, b_rt, tm=512):
    t, d = x.shape
    tm = min(tm, t)
    return pl.pallas_call(
        _router_kernel,
        grid=(t // tm,),
        in_specs=[pl.BlockSpec((tm, d), lambda i: (i, 0)),
                  pl.BlockSpec((ROUTER_ROWS, d), lambda i: (0, 0)),
                  pl.BlockSpec((ROUTER_ROWS, 1), lambda i: (0, 0))],
        out_specs=[pl.BlockSpec((2, tm), lambda i: (0, i)), pl.BlockSpec((2, tm), lambda i: (0, i))],
        out_shape=[jax.ShapeDtypeStruct((2, t), jnp.int32), jax.ShapeDtypeStruct((2, t), F32)],
        compiler_params=_params("parallel"),
        name="moe_router",
    )(x, w_rt, b_rt)


def _sc_gather_rows(table, idx):
    n = idx.shape[0]
    width = table.shape[1]
    mesh = plsc.VectorSubcoreMesh(core_axis_name="core", subcore_axis_name="subcore")

    @pl.kernel(out_type=jax.ShapeDtypeStruct((n, width), table.dtype), mesh=mesh, scratch_types=[])
    def gather(table_hbm, idx_hbm, out_hbm):
        def step(idx_vmem, out_vmem):
            pltpu.sync_copy(table_hbm.at[idx_vmem.at[0]], out_vmem)

        pltpu.emit_pipeline(
            step,
            grid=(n // SC_WINDOW,),
            in_specs=[pl.BlockSpec((1, SC_WINDOW), index_map=lambda i: (0, i))],
            out_specs=[pl.BlockSpec((SC_WINDOW, width), index_map=lambda i: (i, 0))],
            core_axis_name=("core", "subcore"),
            dimension_semantics=(pltpu.PARALLEL,),
        )(idx_hbm, out_hbm)

    return gather(table, idx.reshape(1, n))


def _gather_slabs(slabs, rows):
    n_slab, n_rows, width = slabs.shape
    idx = (jnp.arange(n_slab, dtype=jnp.int32)[:, None] * n_rows + rows[None, :]).reshape(-1)
    return _sc_gather_rows(slabs.reshape(n_slab * n_rows, width), idx).reshape(n_slab, rows.shape[0], width)


def _expert_kernel(block_e_ref, nused_ref, x_ref, wg_ref, wu_ref, wd_ref, y_ref, wg_bf, wu_bf, wd_bf):
    i = pl.program_id(0)

    @pl.when(i < nused_ref[0])
    def _():
        @pl.when((i == 0) | (block_e_ref[i] != block_e_ref[jnp.maximum(i - 1, 0)]))
        def _():
            wg_bf[...] = wg_ref[...].astype(BF16)
            wu_bf[...] = wu_ref[...].astype(BF16)
            wd_bf[...] = wd_ref[...].astype(BF16)

        xb = _from_slabs(x_ref).astype(BF16)
        g = jnp.dot(xb, wg_bf[...], preferred_element_type=F32)
        u = jnp.dot(xb, wu_bf[...], preferred_element_type=F32)
        h = (g * jax.nn.sigmoid(g)) * u
        _to_slabs(y_ref, jnp.dot(h.astype(BF16), wd_bf[...], preferred_element_type=F32))


def _expert_ffn(block_e, nused, xs, wg, wu, wd, layer):
    n_slab, n_rows, width = xs.shape
    d, de = wg.shape[1], wg.shape[2]
    n_blocks = n_rows // MOE_BLOCK
    base = layer * N_EXPERTS

    def rows(i, be, nu):
        return (0, jnp.minimum(i, nu[0] - 1), 0)

    def expert(i, be, nu):
        return (base + be[i], 0, 0)

    return pl.pallas_call(
        _expert_kernel,
        grid_spec=pltpu.PrefetchScalarGridSpec(
            num_scalar_prefetch=2,
            grid=(n_blocks,),
            in_specs=[pl.BlockSpec((n_slab, MOE_BLOCK, width), rows),
                      pl.BlockSpec((None, d, de), expert),
                      pl.BlockSpec((None, d, de), expert),
                      pl.BlockSpec((None, de, d), expert)],
            out_specs=pl.BlockSpec((n_slab, MOE_BLOCK, width), rows),
            scratch_shapes=[pltpu.VMEM((d, de), BF16), pltpu.VMEM((d, de), BF16), pltpu.VMEM((de, d), BF16)]),
        out_shape=jax.ShapeDtypeStruct((n_slab, n_rows, width), F32),
        compiler_params=_params("arbitrary"),
        name="moe_experts",
    )(block_e, nused, xs, wg, wu, wd)


def _dispatch_plan(ids, n_tok):
    n_asg = 2 * n_tok
    n_blocks = n_asg // MOE_BLOCK + N_EXPERTS
    flat_e = ids.reshape(-1)
    order = jnp.argsort(flat_e, stable=True).astype(jnp.int32)
    rank = jnp.argsort(order).astype(jnp.int32)
    counts = jnp.sum((flat_e[:, None] == jnp.arange(N_EXPERTS, dtype=jnp.int32)[None, :]).astype(jnp.int32), axis=0)
    start = jnp.cumsum(counts) - counts
    nblk = (counts + MOE_BLOCK - 1) // MOE_BLOCK
    bend = jnp.cumsum(nblk)
    bstart = bend - nblk
    nused = bend[-1:].astype(jnp.int32)
    bl = jnp.arange(n_blocks, dtype=jnp.int32)
    block_e = jnp.minimum(jnp.sum((bl[:, None] >= bend[None, :]).astype(jnp.int32), axis=1), N_EXPERTS - 1)
    j0 = (bl - bstart[block_e]) * MOE_BLOCK
    nvalid = jnp.where(bl < nused[0], jnp.clip(counts[block_e] - j0, 0, MOE_BLOCK), 0)
    r = jnp.arange(MOE_BLOCK, dtype=jnp.int32)[None, :]
    pos = jnp.clip(start[block_e][:, None] + j0[:, None] + r, 0, n_asg - 1)
    row_tok = jnp.where(r < nvalid[:, None], order[pos], bl[:, None] * MOE_BLOCK + r) % n_tok
    asg_row = bstart[flat_e] * MOE_BLOCK + rank - start[flat_e]
    return block_e.astype(jnp.int32), nused, row_tok.astype(jnp.int32).reshape(-1), asg_row.astype(jnp.int32)


def _moe_outputs(x, x_slabs, w_rt, b_rt, wg, wu, wd, layer):
    n_tok = x.shape[0]
    ids, gates = _router(x, w_rt, b_rt)
    block_e, nused, row_tok, asg_row = _dispatch_plan(ids, n_tok)
    xs = _gather_slabs(x_slabs, row_tok)
    ys = _expert_ffn(block_e, nused, xs, wg, wu, wd, layer)
    return _gather_slabs(ys, asg_row), gates


def _router_weights(w_grp, b_grp, w_exp, b_exp):
    d = w_grp.shape[0]
    pad = 8 - N_GROUPS
    w_rt = jnp.concatenate([w_grp.T, jnp.zeros((pad, d), F32), w_exp.T], axis=0)
    b_rt = jnp.concatenate([b_grp, jnp.zeros((pad,), F32), b_exp]).reshape(ROUTER_ROWS, 1)
    return w_rt.astype(F32), b_rt.astype(F32)


def kernel(x, p, w_qkv_a, w_o_a, w_qkv_b, w_o_b, sinks_b, w_qkv_c, w_o_c, ln1_g, ln1_b, ln2_g, ln2_b, w_grp, b_grp, w_exp, b_exp, w_e_gate, w_e_up, w_e_down, w_ple_gate, w_ple_proj):
    bsz, seq, d = x.shape
    depth = p.shape[0]
    n_tok = bsz * seq
    alpha = (2.0 * depth) ** 0.25
    kv_w = SWA_KV_HEADS * HEAD_DIM
    swa_cols = jnp.asarray(np.concatenate([np.arange(h * HEAD_DIM, (h + 1) * HEAD_DIM) for h in _swa_head_order()]))

    n_all = depth * N_EXPERTS
    wg_all = w_e_gate.reshape(n_all, d, -1)
    wu_all = w_e_up.reshape(n_all, d, -1)
    wd_all = w_e_down.reshape(n_all, -1, d)
    p_all = p.reshape(depth * n_tok, -1)

    xt = x.reshape(n_tok, d)
    for i in range(depth):
        mixer, j = i % 3, i // 3
        if mixer == 0:
            qkv = _proj(xt, w_qkv_a[j].astype(BF16))
            o = _moba_attention(qkv.reshape(bsz, seq, 3 * d), d)
            w_o = w_o_a[j]
        elif mixer == 1:
            w_qkv = jnp.concatenate([w_qkv_b[j][:, :d][:, swa_cols], w_qkv_b[j][:, d:]], axis=1)
            qkv = _proj(xt, w_qkv.astype(BF16))
            o = _swa_attention(qkv.reshape(bsz, seq, d + 2 * kv_w), sinks_b[j], d)
            w_o = w_o_b[j][swa_cols, :]
        else:
            qkv = _proj(xt, w_qkv_c[j].astype(BF16))
            o = _sb_attention(qkv.reshape(bsz, seq, 3 * d), d)
            w_o = w_o_c[j]
        x1, x1_slabs = _oproj_ln(o.reshape(n_tok, d), w_o.astype(BF16), xt, ln1_g[i], ln1_b[i], alpha)

        w_rt, b_rt = _router_weights(w_grp[i], b_grp[i], w_exp[i], b_exp[i])
        m2, gates = _moe_outputs(x1, x1_slabs, w_rt, b_rt, wg_all, wu_all, wd_all, i)
        w0 = jnp.broadcast_to(gates[0][:, None], (n_tok, LANES))
        w1 = jnp.broadcast_to(gates[1][:, None], (n_tok, LANES))
        xt = _ln_ple(x1, m2, w0, w1, ln2_g[i], ln2_b[i], p_all, i,
                     w_ple_gate[i].astype(BF16), w_ple_proj[i].astype(BF16), alpha)
    return xt.reshape(bsz, seq, d)
```

```python
import functools

import jax
import jax.numpy as jnp
import numpy as np
from jax import lax
from jax.experimental import pallas as pl
from jax.experimental.pallas import tpu as pltpu
from jax.experimental.pallas import tpu_sc as plsc

F32 = jnp.float32
BF16 = jnp.bfloat16

LANES = 128
HEAD_DIM = 64
N_HEADS = 16
N_PAIRS = N_HEADS // 2
MOBA_BLOCK = 256
MOBA_TOPK = 3
SWA_WINDOW = 128
SWA_KV_HEADS = 4
SWA_GROUP = N_HEADS // SWA_KV_HEADS
SWA_QWINS = 4
SB_TILE = 512
SB_DEAD_MASS = -100.0
N_GROUPS = 4
EXPERTS_PER_GROUP = 8
N_EXPERTS = N_GROUPS * EXPERTS_PER_GROUP
MOE_BLOCK = 512
LN_EPS = 1e-5
ROUTER_ROWS = 8 + N_EXPERTS
N_SLABS = 4
SC_WINDOW = 128
VMEM_LIMIT = 48 * 1024 * 1024

NT_DIMS = (((1,), (1,)), ((), ()))


def _alibi_slopes():
    return jnp.asarray(np.array([2.0 ** (-8.0 * (h + 1) / N_HEADS) for h in range(N_HEADS)], dtype=np.float32))


def _params(*semantics):
    return pltpu.CompilerParams(dimension_semantics=semantics, vmem_limit_bytes=VMEM_LIMIT)


def _head_masks():
    lane = lax.broadcasted_iota(jnp.int32, (1, LANES), 1)
    return (lane < HEAD_DIM, lane >= HEAD_DIM)


def _proj_kernel(x_ref, w_ref, o_ref):
    o_ref[...] = jnp.dot(x_ref[...].astype(BF16), w_ref[...], preferred_element_type=F32).astype(o_ref.dtype)


def _proj(x, w, tm=1024, tn=512):
    t, k = x.shape
    n = w.shape[1]
    tm = min(tm, t)
    return pl.pallas_call(
        _proj_kernel,
        grid=(t // tm, n // tn),
        in_specs=[pl.BlockSpec((tm, k), lambda i, j: (i, 0)), pl.BlockSpec((k, tn), lambda i, j: (0, j))],
        out_specs=pl.BlockSpec((tm, tn), lambda i, j: (i, j)),
        out_shape=jax.ShapeDtypeStruct((t, n), BF16),
        compiler_params=_params("parallel", "parallel"),
        name="qkv_proj",
    )(x, w)


def _layer_norm(y, g, b):
    mu = jnp.mean(y, axis=-1, keepdims=True)
    d = y - mu
    var = jnp.mean(d * d, axis=-1, keepdims=True)
    return d * lax.rsqrt(var + LN_EPS) * g + b


def _to_slabs(slab_ref, y):
    w = y.shape[1] // N_SLABS
    for c in range(N_SLABS):
        slab_ref[c] = y[:, c * w:(c + 1) * w]


def _from_slabs(slab_ref):
    return jnp.concatenate([slab_ref[c] for c in range(N_SLABS)], axis=1)


def _oproj_ln_kernel(o_ref, w_ref, x_ref, g_ref, b_ref, out_ref, slab_ref, *, alpha):
    h = jnp.dot(o_ref[...], w_ref[...], preferred_element_type=F32)
    y = _layer_norm(alpha * x_ref[...] + h, g_ref[...], b_ref[...])
    out_ref[...] = y
    _to_slabs(slab_ref, y)


def _oproj_ln(o, w, x, g, b, alpha, tm=512):
    t, d = x.shape
    tm = min(tm, t)
    row = pl.BlockSpec((tm, d), lambda i: (i, 0))
    vec = pl.BlockSpec((1, d), lambda i: (0, 0))
    return pl.pallas_call(
        functools.partial(_oproj_ln_kernel, alpha=alpha),
        grid=(t // tm,),
        in_specs=[row, pl.BlockSpec((d, d), lambda i: (0, 0)), row, vec, vec],
        out_specs=[row, pl.BlockSpec((N_SLABS, tm, d // N_SLABS), lambda i: (0, i, 0))],
        out_shape=[jax.ShapeDtypeStruct((t, d), F32), jax.ShapeDtypeStruct((N_SLABS, t, d // N_SLABS), F32)],
        compiler_params=_params("parallel"),
        name="oproj_ln",
    )(o, w, x, g.reshape(1, d), b.reshape(1, d))


def _ln_ple_kernel(x_ref, m0_ref, m1_ref, w0_ref, w1_ref, g_ref, b_ref, p_ref, wg_ref, wp_ref, out_ref, *, alpha):
    m = _from_slabs(m0_ref) * w0_ref[:, 0:1] + _from_slabs(m1_ref) * w1_ref[:, 0:1]
    x2 = _layer_norm(alpha * x_ref[...] + m, g_ref[...], b_ref[...])
    gate = jax.nn.sigmoid(jnp.dot(x2.astype(BF16), wg_ref[...], preferred_element_type=F32))
    proj = jnp.dot(p_ref[...].astype(BF16), wp_ref[...], preferred_element_type=F32)
    out_ref[...] = x2 + gate * proj


def _ln_ple(x, m2, w0, w1, g, b, p, layer, wg, wp, alpha, tm=512):
    t, d = x.shape
    pd = p.shape[1]
    tm = min(tm, t)
    nt = t // tm
    p0 = layer * nt
    row = pl.BlockSpec((tm, d), lambda i: (i, 0))
    rep = pl.BlockSpec((tm, LANES), lambda i: (i, 0))
    vec = pl.BlockSpec((1, d), lambda i: (0, 0))
    return pl.pallas_call(
        functools.partial(_ln_ple_kernel, alpha=alpha),
        grid=(nt,),
        in_specs=[row, pl.BlockSpec((N_SLABS, tm, d // N_SLABS), lambda i: (0, i, 0)),
                  pl.BlockSpec((N_SLABS, tm, d // N_SLABS), lambda i: (0, i + nt, 0)), rep, rep, vec, vec,
                  pl.BlockSpec((tm, pd), lambda i: (p0 + i, 0)),
                  pl.BlockSpec((d, d), lambda i: (0, 0)), pl.BlockSpec((pd, d), lambda i: (0, 0))],
        out_specs=row,
        out_shape=jax.ShapeDtypeStruct((t, d), F32),
        compiler_params=_params("parallel"),
        name="ln_ple",
    )(x, m2, m2, w0, w1, g.reshape(1, d), b.reshape(1, d), p, wg, wp)


def _moba_kernel(slopes_ref, q_ref, k_ref, v_ref, o_ref, kmean_ref, vt_ref, bias_ref, *, n_blk, scale):
    pair = pl.program_id(1)
    qi = pl.program_id(2)
    blk = MOBA_BLOCK

    @pl.when(qi == 0)
    def _():
        for j in range(n_blk):
            kb = k_ref[j * blk:(j + 1) * blk, :].astype(F32)
            kmean_ref[j:j + 1, :] = jnp.sum(kb, axis=0, keepdims=True) * (1.0 / blk)
            vt_ref[j] = v_ref[j * blk:(j + 1) * blk, :].astype(F32).T.astype(BF16)

    kmean = kmean_ref[...].astype(BF16)
    krow = lax.broadcasted_iota(jnp.int32, (blk, blk), 0)
    qcol = lax.broadcasted_iota(jnp.int32, (blk, blk), 1)
    dpos = (qcol - krow).astype(F32)
    jrow = lax.broadcasted_iota(jnp.int32, (n_blk, blk), 0)
    past = jrow < qi
    q2 = q_ref[...]

    heads = []
    for hh, hmask in enumerate(_head_masks()):
        slope = slopes_ref[2 * pair + hh]
        qh = jnp.where(hmask, q2, jnp.zeros_like(q2))

        gate = lax.dot_general(kmean, qh, NT_DIMS, preferred_element_type=F32)
        sel = jnp.zeros((n_blk, blk), F32)
        for j in range(n_blk):
            gj = gate[j:j + 1, :]
            beats = past & ((gate > gj) | ((gate == gj) & (jrow < j)))
            n_beat = jnp.sum(beats.astype(F32), axis=0, keepdims=True)
            sel = jnp.where(jrow == j, (n_beat < MOBA_TOPK).astype(F32), sel)
        bias_ref[hh] = jnp.where(past & (sel > 0.5), 0.0, -jnp.inf)
        qs = qh * jnp.asarray(scale, BF16)
        heads.append((qs, slope * dpos, slope))

    def scores(j):
        kb = k_ref[pl.ds(pl.multiple_of(j * blk, blk), blk), :]
        return [lax.dot_general(kb, qs, NT_DIMS, preferred_element_type=F32) - sd for qs, sd, _ in heads]

    def absorb(s, j, hh, m, l, acc):
        vt = vt_ref[j][hh * HEAD_DIM:(hh + 1) * HEAD_DIM]
        m_new = jnp.maximum(m, jnp.max(s, axis=0, keepdims=True))
        shift = jnp.where(m_new == -jnp.inf, 0.0, m_new)
        a = jnp.exp(m - shift)
        p = jnp.exp(s - shift)
        l = a * l + jnp.sum(p, axis=0, keepdims=True)
        acc = a * acc + jnp.dot(vt, p.astype(BF16), preferred_element_type=F32)
        return m_new, l, acc

    def past_block(j, carry):
        new = list(scores(j + 1))
        for hh, (_, _, slope) in enumerate(heads):
            m, l, acc = carry[2 + 3 * hh:5 + 3 * hh]
            rb = bias_ref[hh, pl.ds(j, 1), :] - slope * ((qi - j) * blk).astype(F32)
            new += list(absorb(carry[hh] + rb, j, hh, m, l, acc))
        return tuple(new)

    init = list(scores(0))
    for _ in heads:
        init += [jnp.full((1, blk), -jnp.inf, F32), jnp.zeros((1, blk), F32), jnp.zeros((HEAD_DIM, blk), F32)]
    carry = lax.fori_loop(0, qi // 2, lambda t, c: past_block(2 * t + 1, past_block(2 * t, c)), tuple(init))
    carry = lax.cond(qi % 2 == 1, lambda c: past_block(qi - 1, c), lambda c: c, carry)
    outs = []
    for hh in range(len(heads)):
        m, l, acc = carry[2 + 3 * hh:5 + 3 * hh]
        m, l, acc = absorb(jnp.where(dpos >= 0, carry[hh], -jnp.inf), qi, hh, m, l, acc)
        outs.append(acc / l)
    o_ref[...] = jnp.concatenate(outs, axis=0).T.astype(o_ref.dtype)


def _moba_attention(qkv, d):
    bsz, seq, _ = qkv.shape
    assert seq % MOBA_BLOCK == 0
    n_blk = seq // MOBA_BLOCK
    ncol = d // LANES
    return pl.pallas_call(
        functools.partial(_moba_kernel, n_blk=n_blk, scale=HEAD_DIM ** -0.5),
        grid=(bsz, N_PAIRS, n_blk),
        in_specs=[pl.BlockSpec(memory_space=pltpu.SMEM),
                  pl.BlockSpec((None, MOBA_BLOCK, LANES), lambda b, p, i: (b, i, p)),
                  pl.BlockSpec((None, seq, LANES), lambda b, p, i: (b, 0, ncol + p)),
                  pl.BlockSpec((None, seq, LANES), lambda b, p, i: (b, 0, 2 * ncol + p))],
        out_specs=pl.BlockSpec((None, MOBA_BLOCK, LANES), lambda b, p, i: (b, i, p)),
        out_shape=jax.ShapeDtypeStruct((bsz, seq, d), BF16),
        scratch_shapes=[pltpu.VMEM((n_blk, LANES), F32), pltpu.VMEM((n_blk, LANES, MOBA_BLOCK), BF16),
                        pltpu.VMEM((2, n_blk, MOBA_BLOCK), F32)],
        compiler_params=_params("parallel", "parallel", "arbitrary"),
        name="moba_attn",
    )(_alibi_slopes(), qkv, qkv, qkv)


def _swa_head_order():
    order = []
    for pair in range(N_PAIRS):
        r, i = divmod(pair, SWA_GROUP)
        order += [(2 * r) * SWA_GROUP + i, (2 * r + 1) * SWA_GROUP + i]
    return order


def _swa_kernel(slopes_ref, sinks_ref, q_ref, k_ref, v_ref, o_ref, vt_ref, *, scale, n_win):
    pair = pl.program_id(1)
    n = pl.program_id(2)
    win = SWA_WINDOW
    r = pair // SWA_GROUP
    i = pair % SWA_GROUP

    @pl.when(n == 0)
    def _():
        for j in range(n_win):
            vt_ref[j] = v_ref[j * win:(j + 1) * win, :].astype(F32).T.astype(BF16)

    w0 = n * SWA_QWINS
    q0 = pl.multiple_of(w0 * win, win)
    before0 = pl.multiple_of(jnp.maximum(w0 - 1, 0) * win, win)
    k_blocks = [k_ref[pl.ds(before0, win), :]] + [k_ref[pl.ds(q0 + w * win, win), :] for w in range(SWA_QWINS)]
    vt_blocks = [vt_ref[jnp.maximum(w0 - 1, 0)]] + [vt_ref[w0 + w] for w in range(SWA_QWINS)]
    krow = lax.broadcasted_iota(jnp.int32, (2 * win, win), 0)
    qcol = lax.broadcasted_iota(jnp.int32, (2 * win, win), 1)
    dist = (qcol + win - krow).astype(F32)
    in_window = (krow > qcol) & (krow <= qcol + win)

    per_head = []
    for hh in range(2):
        head = (2 * r + hh) * SWA_GROUP + i
        per_head.append((slopes_ref[head] * dist, sinks_ref[head]))

    for w in range(SWA_QWINS):
        q2 = q_ref[w * win:(w + 1) * win, :]
        k2 = jnp.concatenate([k_blocks[w], k_blocks[w + 1]], axis=0)
        vt2 = jnp.concatenate([vt_blocks[w], vt_blocks[w + 1]], axis=1)
        allowed = in_window if w > 0 else in_window & ((krow >= win) | (n > 0))
        outs = []
        for hh, (hmask, (bias, sink)) in enumerate(zip(_head_masks(), per_head)):
            qs = jnp.where(hmask, q2, jnp.zeros_like(q2)) * jnp.asarray(scale, BF16)
            s = lax.dot_general(k2, qs, NT_DIMS, preferred_element_type=F32) - bias
            s = jnp.where(allowed, s, -jnp.inf)
            m = jnp.maximum(jnp.max(s, axis=0, keepdims=True), sink)
            p = jnp.exp(s - m)
            l = jnp.sum(p, axis=0, keepdims=True) + jnp.exp(sink - m)
            o_t = jnp.dot(vt2[hh * HEAD_DIM:(hh + 1) * HEAD_DIM], p.astype(BF16), preferred_element_type=F32)
            outs.append(o_t / l)
        o_ref[w * win:(w + 1) * win, :] = jnp.concatenate(outs, axis=0).T.astype(o_ref.dtype)


def _swa_attention(qkv, sinks, d):
    bsz, seq, _ = qkv.shape
    qb = SWA_QWINS * SWA_WINDOW
    assert seq % qb == 0
    n_win = seq // SWA_WINDOW
    ncol = d // LANES
    kv_col = SWA_KV_HEADS * HEAD_DIM // LANES
    return pl.pallas_call(
        functools.partial(_swa_kernel, scale=HEAD_DIM ** -0.5, n_win=n_win),
        grid=(bsz, N_PAIRS, seq // qb),
        in_specs=[pl.BlockSpec(memory_space=pltpu.SMEM), pl.BlockSpec(memory_space=pltpu.SMEM),
                  pl.BlockSpec((None, qb, LANES), lambda b, p, i: (b, i, p)),
                  pl.BlockSpec((None, seq, LANES), lambda b, p, i: (b, 0, ncol + p // SWA_GROUP)),
                  pl.BlockSpec((None, seq, LANES), lambda b, p, i: (b, 0, ncol + kv_col + p // SWA_GROUP))],
        out_specs=pl.BlockSpec((None, qb, LANES), lambda b, p, i: (b, i, p)),
        out_shape=jax.ShapeDtypeStruct((bsz, seq, d), BF16),
        scratch_shapes=[pltpu.VMEM((n_win, LANES, SWA_WINDOW), BF16)],
        compiler_params=_params("parallel", "parallel", "arbitrary"),
        name="swa_attn",
    )(_alibi_slopes(), sinks.astype(F32), qkv, qkv, qkv)


def _log_sigmoid(z):
    return jnp.minimum(z, 0.0) - jnp.log(1.0 + jnp.exp(-jnp.abs(z)))


def _suffix_sums(lk, upper):
    hi = lk.astype(BF16)
    lo = (lk - hi.astype(F32)).astype(BF16)
    n = lk.shape[0]
    both = jnp.dot(jnp.concatenate([hi, lo], axis=0), upper, preferred_element_type=F32)
    return both[:n] + both[n:]


def _sb_tile(qh, kb, vb, upper, later, strict):
    z = lax.dot_general(qh, kb, NT_DIMS, preferred_element_type=F32)
    ls = _log_sigmoid(z)
    lk = ls - z
    if strict is not None:
        lk = jnp.where(strict, lk, 0.0)
    after = _suffix_sums(lk, upper)
    if later is not None:
        after = after + later
    a = jnp.exp(ls + after)
    if strict is not None:
        a = jnp.where(strict, a, 0.0)
    mass = jnp.sum(lk, axis=1, keepdims=True)
    return jnp.dot(a.astype(BF16), vb, preferred_element_type=F32), (mass if later is None else later + mass)


def _sb_kernel(q_ref, k_ref, v_ref, o_ref, *, scale):
    c = pl.program_id(2)
    blk = SB_TILE
    row = lax.broadcasted_iota(jnp.int32, (blk, blk), 0)
    col = lax.broadcasted_iota(jnp.int32, (blk, blk), 1)
    strict = col < row
    upper = (row > col).astype(BF16)
    own0 = pl.multiple_of(c * blk, blk)
    k_own = k_ref[pl.ds(own0, blk), :]
    v_own = v_ref[pl.ds(own0, blk), :]
    q2 = q_ref[...]
    masks = _head_masks()
    qhs = [jnp.where(hmask, q2, jnp.zeros_like(q2)) * jnp.asarray(scale, BF16) for hmask in masks]

    carry = []
    for qh in qhs:
        acc, later = _sb_tile(qh, k_own, v_own, upper, None, strict)
        carry += [later, acc]

    def earlier_block(state):
        step, carry = state[0], state[1:]
        j0 = pl.multiple_of((c - 1 - step) * blk, blk)
        kb = k_ref[pl.ds(j0, blk), :]
        vb = v_ref[pl.ds(j0, blk), :]
        new = [step + 1]
        for hh, qh in enumerate(qhs):
            contrib, later = _sb_tile(qh, kb, vb, upper, carry[2 * hh], None)
            new += [later, carry[2 * hh + 1] + contrib]
        return tuple(new)

    def weights_can_be_nonzero(state):
        alive = jnp.maximum(jnp.max(state[1]), jnp.max(state[3])) >= SB_DEAD_MASS
        return (state[0] < c) & alive

    carry = lax.while_loop(weights_can_be_nonzero, earlier_block, (jnp.int32(0),) + tuple(carry))[1:]
    o_ref[...] = jnp.where(masks[0], carry[1], carry[3]).astype(o_ref.dtype)


def _sb_attention(qkv, d):
    bsz, seq, _ = qkv.shape
    blk = min(SB_TILE, seq)
    assert seq % blk == 0 and blk == SB_TILE
    ncol = d // LANES
    return pl.pallas_call(
        functools.partial(_sb_kernel, scale=HEAD_DIM ** -0.5),
        grid=(bsz, N_PAIRS, seq // blk),
        in_specs=[pl.BlockSpec((None, blk, LANES), lambda b, p, i: (b, i, p)),
                  pl.BlockSpec((None, seq, LANES), lambda b, p, i: (b, 0, ncol + p)),
                  pl.BlockSpec((None, seq, LANES), lambda b, p, i: (b, 0, 2 * ncol + p))],
        out_specs=pl.BlockSpec((None, blk, LANES), lambda b, p, i: (b, i, p)),
        out_shape=jax.ShapeDtypeStruct((bsz, seq, d), BF16),
        compiler_params=_params("parallel", "parallel", "arbitrary"),
        name="sb_attn",
    )(qkv, qkv, qkv)


def _split_bf16(a):
    hi = a.astype(BF16)
    return hi, (a - hi.astype(F32)).astype(BF16)


def _router_kernel(x_ref, w_ref, b_ref, ids_ref, gates_ref):
    xh, xl = _split_bf16(x_ref[...])
    wh, wl = _split_bf16(w_ref[...])
    logits = (lax.dot_general(wh, xh, NT_DIMS, preferred_element_type=F32)
              + lax.dot_general(wh, xl, NT_DIMS, preferred_element_type=F32)
              + lax.dot_general(wl, xh, NT_DIMS, preferred_element_type=F32)) + b_ref[...]
    tm = logits.shape[1]
    gl = logits[0:N_GROUPS, :]
    g_row = lax.broadcasted_iota(jnp.int32, (N_GROUPS, tm), 0).astype(F32)
    g_max = jnp.max(gl, axis=0, keepdims=True)
    g_p = 1.0 / jnp.sum(jnp.exp(gl - g_max), axis=0, keepdims=True)
    g_idx = jnp.min(jnp.where(gl == g_max, g_row, float(N_GROUPS)), axis=0, keepdims=True)

    in_grp = jnp.zeros((EXPERTS_PER_GROUP, tm), F32)
    for g in range(N_GROUPS):
        lo = 8 + g * EXPERTS_PER_GROUP
        in_grp = jnp.where(g_idx == float(g), logits[lo:lo + EXPERTS_PER_GROUP, :], in_grp)
    e_exp = jnp.exp(in_grp - jnp.max(in_grp, axis=0, keepdims=True))
    e_p = e_exp / jnp.sum(e_exp, axis=0, keepdims=True)
    e_row = lax.broadcasted_iota(jnp.int32, (EXPERTS_PER_GROUP, tm), 0).astype(F32)
    p1 = jnp.max(e_p, axis=0, keepdims=True)
    i1 = jnp.min(jnp.where(e_p == p1, e_row, float(EXPERTS_PER_GROUP)), axis=0, keepdims=True)
    rest = jnp.where(e_row == i1, -1.0, e_p)
    p2 = jnp.max(rest, axis=0, keepdims=True)
    i2 = jnp.min(jnp.where(rest == p2, e_row, float(EXPERTS_PER_GROUP)), axis=0, keepdims=True)
    den = p1 + p2
    base = g_idx * float(EXPERTS_PER_GROUP)
    ids_ref[0:1, :] = (base + i1).astype(jnp.int32)
    ids_ref[1:2, :] = (base + i2).astype(jnp.int32)
    gates_ref[0:1, :] = g_p * (p1 / den)
    gates_ref[1:2, :] = g_p * (p2 / den)


def _router(x, w_rt, b_rt, tm=512):
    t, d = x.shape
    tm = min(tm, t)
    return pl.pallas_call(
        _router_kernel,
        grid=(t // tm,),
        in_specs=[pl.BlockSpec((tm, d), lambda i: (i, 0)),
                  pl.BlockSpec((ROUTER_ROWS, d), lambda i: (0, 0)),
                  pl.BlockSpec((ROUTER_ROWS, 1), lambda i: (0, 0))],
        out_specs=[pl.BlockSpec((2, tm), lambda i: (0, i)), pl.BlockSpec((2, tm), lambda i: (0, i))],
        out_shape=[jax.ShapeDtypeStruct((2, t), jnp.int32), jax.ShapeDtypeStruct((2, t), F32)],
        compiler_params=_params("parallel"),
        name="moe_router",
    )(x, w_rt, b_rt)


def _sc_gather_rows(table, idx):
    n = idx.shape[0]
    width = table.shape[1]
    mesh = plsc.VectorSubcoreMesh(core_axis_name="core", subcore_axis_name="subcore")

    @pl.kernel(out_type=jax.ShapeDtypeStruct((n, width), table.dtype), mesh=mesh, scratch_types=[])
    def gather(table_hbm, idx_hbm, out_hbm):
        def step(idx_vmem, out_vmem):
            pltpu.sync_copy(table_hbm.at[idx_vmem.at[0]], out_vmem)

        pltpu.emit_pipeline(
            step,
            grid=(n // SC_WINDOW,),
            in_specs=[pl.BlockSpec((1, SC_WINDOW), index_map=lambda i: (0, i))],
            out_specs=[pl.BlockSpec((SC_WINDOW, width), index_map=lambda i: (i, 0))],
            core_axis_name=("core", "subcore"),
            dimension_semantics=(pltpu.PARALLEL,),
        )(idx_hbm, out_hbm)

    return gather(table, idx.reshape(1, n))


def _gather_slabs(slabs, rows):
    n_slab, n_rows, width = slabs.shape
    idx = (jnp.arange(n_slab, dtype=jnp.int32)[:, None] * n_rows + rows[None, :]).reshape(-1)
    return _sc_gather_rows(slabs.reshape(n_slab * n_rows, width), idx).reshape(n_slab, rows.shape[0], width)


def _expert_kernel(block_e_ref, nused_ref, x_ref, wg_ref, wu_ref, wd_ref, y_ref, wg_bf, wu_bf, wd_bf):
    i = pl.program_id(0)

    @pl.when(i < nused_ref[0])
    def _():
        @pl.when((i == 0) | (block_e_ref[i] != block_e_ref[jnp.maximum(i - 1, 0)]))
        def _():
            wg_bf[...] = wg_ref[...].astype(BF16)
            wu_bf[...] = wu_ref[...].astype(BF16)
            wd_bf[...] = wd_ref[...].astype(BF16)

        xb = _from_slabs(x_ref).astype(BF16)
        g = jnp.dot(xb, wg_bf[...], preferred_element_type=F32)
        u = jnp.dot(xb, wu_bf[...], preferred_element_type=F32)
        h = (g * jax.nn.sigmoid(g)) * u
        _to_slabs(y_ref, jnp.dot(h.astype(BF16), wd_bf[...], preferred_element_type=F32))


def _expert_ffn(block_e, nused, xs, wg, wu, wd, layer):
    n_slab, n_rows, width = xs.shape
    d, de = wg.shape[1], wg.shape[2]
    n_blocks = n_rows // MOE_BLOCK
    base = layer * N_EXPERTS

    def rows(i, be, nu):
        return (0, jnp.minimum(i, nu[0] - 1), 0)

    def expert(i, be, nu):
        return (base + be[i], 0, 0)

    return pl.pallas_call(
        _expert_kernel,
        grid_spec=pltpu.PrefetchScalarGridSpec(
            num_scalar_prefetch=2,
            grid=(n_blocks,),
            in_specs=[pl.BlockSpec((n_slab, MOE_BLOCK, width), rows),
                      pl.BlockSpec((None, d, de), expert),
                      pl.BlockSpec((None, d, de), expert),
                      pl.BlockSpec((None, de, d), expert)],
            out_specs=pl.BlockSpec((n_slab, MOE_BLOCK, width), rows),
            scratch_shapes=[pltpu.VMEM((d, de), BF16), pltpu.VMEM((d, de), BF16), pltpu.VMEM((de, d), BF16)]),
        out_shape=jax.ShapeDtypeStruct((n_slab, n_rows, width), F32),
        compiler_params=_params("arbitrary"),
        name="moe_experts",
    )(block_e, nused, xs, wg, wu, wd)


def _dispatch_plan(ids, n_tok):
    n_asg = 2 * n_tok
    n_blocks = n_asg // MOE_BLOCK + N_EXPERTS
    flat_e = ids.reshape(-1)
    order = jnp.argsort(flat_e, stable=True).astype(jnp.int32)
    rank = jnp.argsort(order).astype(jnp.int32)
    counts = jnp.sum((flat_e[:, None] == jnp.arange(N_EXPERTS, dtype=jnp.int32)[None, :]).astype(jnp.int32), axis=0)
    start = jnp.cumsum(counts) - counts
    nblk = (counts + MOE_BLOCK - 1) // MOE_BLOCK
    bend = jnp.cumsum(nblk)
    bstart = bend - nblk
    nused = bend[-1:].astype(jnp.int32)
    bl = jnp.arange(n_blocks, dtype=jnp.int32)
    block_e = jnp.minimum(jnp.sum((bl[:, None] >= bend[None, :]).astype(jnp.int32), axis=1), N_EXPERTS - 1)
    j0 = (bl - bstart[block_e]) * MOE_BLOCK
    nvalid = jnp.where(bl < nused[0], jnp.clip(counts[block_e] - j0, 0, MOE_BLOCK), 0)
    r = jnp.arange(MOE_BLOCK, dtype=jnp.int32)[None, :]
    pos = jnp.clip(start[block_e][:, None] + j0[:, None] + r, 0, n_asg - 1)
    row_tok = jnp.where(r < nvalid[:, None], order[pos], bl[:, None] * MOE_BLOCK + r) % n_tok
    asg_row = bstart[flat_e] * MOE_BLOCK + rank - start[flat_e]
    return block_e.astype(jnp.int32), nused, row_tok.astype(jnp.int32).reshape(-1), asg_row.astype(jnp.int32)


def _moe_outputs(x, x_slabs, w_rt, b_rt, wg, wu, wd, layer):
    n_tok = x.shape[0]
    ids, gates = _router(x, w_rt, b_rt)
    block_e, nused, row_tok, asg_row = _dispatch_plan(ids, n_tok)
    xs = _gather_slabs(x_slabs, row_tok)
    ys = _expert_ffn(block_e, nused, xs, wg, wu, wd, layer)
    return _gather_slabs(ys, asg_row), gates


def _router_weights(w_grp, b_grp, w_exp, b_exp):
    d = w_grp.shape[0]
    pad = 8 - N_GROUPS
    w_rt = jnp.concatenate([w_grp.T, jnp.zeros((pad, d), F32), w_exp.T], axis=0)
    b_rt = jnp.concatenate([b_grp, jnp.zeros((pad,), F32), b_exp]).reshape(ROUTER_ROWS, 1)
    return w_rt.astype(F32), b_rt.astype(F32)


def kernel(x, p, w_qkv_a, w_o_a, w_qkv_b, w_o_b, sinks_b, w_qkv_c, w_o_c, ln1_g, ln1_b, ln2_g, ln2_b, w_grp, b_grp, w_exp, b_exp, w_e_gate, w_e_up, w_e_down, w_ple_gate, w_ple_proj):
    bsz, seq, d = x.shape
    depth = p.shape[0]
    n_tok = bsz * seq
    alpha = (2.0 * depth) ** 0.25
    kv_w = SWA_KV_HEADS * HEAD_DIM
    swa_cols = jnp.asarray(np.concatenate([np.arange(h * HEAD_DIM, (h + 1) * HEAD_DIM) for h in _swa_head_order()]))

    n_all = depth * N_EXPERTS
    wg_all = w_e_gate.reshape(n_all, d, -1)
    wu_all = w_e_up.reshape(n_all, d, -1)
    wd_all = w_e_down.reshape(n_all, -1, d)
    p_all = p.reshape(depth * n_tok, -1)

    xt = x.reshape(n_tok, d)
    for i in range(depth):
        mixer, j = i % 3, i // 3
        if mixer == 0:
            qkv = _proj(xt, w_qkv_a[j].astype(BF16))
            o = _moba_attention(qkv.reshape(bsz, seq, 3 * d), d)
            w_o = w_o_a[j]
        elif mixer == 1:
            w_qkv = jnp.concatenate([w_qkv_b[j][:, :d][:, swa_cols], w_qkv_b[j][:, d:]], axis=1)
            qkv = _proj(xt, w_qkv.astype(BF16))
            o = _swa_attention(qkv.reshape(bsz, seq, d + 2 * kv_w), sinks_b[j], d)
            w_o = w_o_b[j][swa_cols, :]
        else:
            qkv = _proj(xt, w_qkv_c[j].astype(BF16))
            o = _sb_attention(qkv.reshape(bsz, seq, 3 * d), d)
            w_o = w_o_c[j]
        x1, x1_slabs = _oproj_ln(o.reshape(n_tok, d), w_o.astype(BF16), xt, ln1_g[i], ln1_b[i], alpha)

        w_rt, b_rt = _router_weights(w_grp[i], b_grp[i], w_exp[i], b_exp[i])
        m2, gates = _moe_outputs(x1, x1_slabs, w_rt, b_rt, wg_all, wu_all, wd_all, i)
        w0 = jnp.broadcast_to(gates[0][:, None], (n_tok, LANES))
        w1 = jnp.broadcast_to(gates[1][:, None], (n_tok, LANES))
        xt = _ln_ple(x1, m2, w0, w1, ln2_g[i], ln2_b[i], p_all, i,
                     w_ple_gate[i].astype(BF16), w_ple_proj[i].astype(BF16), alpha)
    return xt.reshape(bsz, seq, d)
```

```python
import functools

import jax
import jax.numpy as jnp
import numpy as np
from jax import lax
from jax.experimental import pallas as pl
from jax.experimental.pallas import tpu as pltpu
from jax.experimental.pallas import tpu_sc as plsc

F32 = jnp.float32
BF16 = jnp.bfloat16

LANES = 128
HEAD_DIM = 64
N_HEADS = 16
N_PAIRS = N_HEADS // 2
MOBA_BLOCK = 256
MOBA_TOPK = 3
SWA_WINDOW = 128
SWA_KV_HEADS = 4
SWA_GROUP = N_HEADS // SWA_KV_HEADS
SWA_QWINS = 4
SB_TILE = 512
SB_DEAD_MASS = -100.0
N_GROUPS = 4
EXPERTS_PER_GROUP = 8
N_EXPERTS = N_GROUPS * EXPERTS_PER_GROUP
MOE_BLOCK = 512
LN_EPS = 1e-5
ROUTER_ROWS = 8 + N_EXPERTS
N_SLABS = 4
SC_WINDOW = 128
VMEM_LIMIT = 48 * 1024 * 1024

NT_DIMS = (((1,), (1,)), ((), ()))


def _alibi_slopes():
    return jnp.asarray(np.array([2.0 ** (-8.0 * (h + 1) / N_HEADS) for h in range(N_HEADS)], dtype=np.float32))


def _params(*semantics):
    return pltpu.CompilerParams(dimension_semantics=semantics, vmem_limit_bytes=VMEM_LIMIT)


def _head_masks():
    lane = lax.broadcasted_iota(jnp.int32, (1, LANES), 1)
    return (lane < HEAD_DIM, lane >= HEAD_DIM)


def _proj_kernel(x_ref, w_ref, o_ref):
    o_ref[...] = jnp.dot(x_ref[...].astype(BF16), w_ref[...], preferred_element_type=F32).astype(o_ref.dtype)


def _proj(x, w, tm=1024, tn=512):
    t, k = x.shape
    n = w.shape[1]
    tm = min(tm, t)
    return pl.pallas_call(
        _proj_kernel,
        grid=(t // tm, n // tn),
        in_specs=[pl.BlockSpec((tm, k), lambda i, j: (i, 0)), pl.BlockSpec((k, tn), lambda i, j: (0, j))],
        out_specs=pl.BlockSpec((tm, tn), lambda i, j: (i, j)),
        out_shape=jax.ShapeDtypeStruct((t, n), BF16),
        compiler_params=_params("parallel", "parallel"),
        name="qkv_proj",
    )(x, w)


def _layer_norm(y, g, b):
    mu = jnp.mean(y, axis=-1, keepdims=True)
    d = y - mu
    var = jnp.mean(d * d, axis=-1, keepdims=True)
    return d * lax.rsqrt(var + LN_EPS) * g + b


def _to_slabs(slab_ref, y):
    w = y.shape[1] // N_SLABS
    for c in range(N_SLABS):
        slab_ref[c] = y[:, c * w:(c + 1) * w]


def _from_slabs(slab_ref):
    return jnp.concatenate([slab_ref[c] for c in range(N_SLABS)], axis=1)


def _oproj_ln_kernel(o_ref, w_ref, x_ref, g_ref, b_ref, out_ref, slab_ref, *, alpha):
    h = jnp.dot(o_ref[...], w_ref[...], preferred_element_type=F32)
    y = _layer_norm(alpha * x_ref[...] + h, g_ref[...], b_ref[...])
    out_ref[...] = y
    _to_slabs(slab_ref, y)


def _oproj_ln(o, w, x, g, b, alpha, tm=512):
    t, d = x.shape
    tm = min(tm, t)
    row = pl.BlockSpec((tm, d), lambda i: (i, 0))
    vec = pl.BlockSpec((1, d), lambda i: (0, 0))
    return pl.pallas_call(
        functools.partial(_oproj_ln_kernel, alpha=alpha),
        grid=(t // tm,),
        in_specs=[row, pl.BlockSpec((d, d), lambda i: (0, 0)), row, vec, vec],
        out_specs=[row, pl.BlockSpec((N_SLABS, tm, d // N_SLABS), lambda i: (0, i, 0))],
        out_shape=[jax.ShapeDtypeStruct((t, d), F32), jax.ShapeDtypeStruct((N_SLABS, t, d // N_SLABS), F32)],
        compiler_params=_params("parallel"),
        name="oproj_ln",
    )(o, w, x, g.reshape(1, d), b.reshape(1, d))


def _ln_ple_kernel(x_ref, m0_ref, m1_ref, w0_ref, w1_ref, g_ref, b_ref, p_ref, wg_ref, wp_ref, out_ref, *, alpha):
    m = _from_slabs(m0_ref) * w0_ref[:, 0:1] + _from_slabs(m1_ref) * w1_ref[:, 0:1]
    x2 = _layer_norm(alpha * x_ref[...] + m, g_ref[...], b_ref[...])
    gate = jax.nn.sigmoid(jnp.dot(x2.astype(BF16), wg_ref[...], preferred_element_type=F32))
    proj = jnp.dot(p_ref[...].astype(BF16), wp_ref[...], preferred_element_type=F32)
    out_ref[...] = x2 + gate * proj


def _ln_ple(x, m2, w0, w1, g, b, p, layer, wg, wp, alpha, tm=512):
    t, d = x.shape
    pd = p.shape[1]
    tm = min(tm, t)
    nt = t // tm
    p0 = layer * nt
    row = pl.BlockSpec((tm, d), lambda i: (i, 0))
    rep = pl.BlockSpec((tm, LANES), lambda i: (i, 0))
    vec = pl.BlockSpec((1, d), lambda i: (0, 0))
    return pl.pallas_call(
        functools.partial(_ln_ple_kernel, alpha=alpha),
        grid=(nt,),
        in_specs=[row, pl.BlockSpec((N_SLABS, tm, d // N_SLABS), lambda i: (0, i, 0)),
                  pl.BlockSpec((N_SLABS, tm, d // N_SLABS), lambda i: (0, i + nt, 0)), rep, rep, vec, vec,
                  pl.BlockSpec((tm, pd), lambda i: (p0 + i, 0)),
                  pl.BlockSpec((d, d), lambda i: (0, 0)), pl.BlockSpec((pd, d), lambda i: (0, 0))],
        out_specs=row,
        out_shape=jax.ShapeDtypeStruct((t, d), F32),
        compiler_params=_params("parallel"),
        name="ln_ple",
    )(x, m2, m2, w0, w1, g.reshape(1, d), b.reshape(1, d), p, wg, wp)


def _moba_kernel(slopes_ref, q_ref, k_ref, v_ref, o_ref, kmean_ref, vt_ref, bias_ref, *, n_blk, scale):
    pair = pl.program_id(1)
    qi = pl.program_id(2)
    blk = MOBA_BLOCK

    @pl.when(qi == 0)
    def _():
        for j in range(n_blk):
            kb = k_ref[j * blk:(j + 1) * blk, :].astype(F32)
            kmean_ref[j:j + 1, :] = jnp.sum(kb, axis=0, keepdims=True) * (1.0 / blk)
            vt_ref[j] = v_ref[j * blk:(j + 1) * blk, :].astype(F32).T.astype(BF16)

    kmean = kmean_ref[...].astype(BF16)
    krow = lax.broadcasted_iota(jnp.int32, (blk, blk), 0)
    qcol = lax.broadcasted_iota(jnp.int32, (blk, blk), 1)
    dpos = (qcol - krow).astype(F32)
    jrow = lax.broadcasted_iota(jnp.int32, (n_blk, blk), 0)
    past = jrow < qi
    q2 = q_ref[...]

    heads = []
    for hh, hmask in enumerate(_head_masks()):
        slope = slopes_ref[2 * pair + hh]
        qh = jnp.where(hmask, q2, jnp.zeros_like(q2))

        gate = lax.dot_general(kmean, qh, NT_DIMS, preferred_element_type=F32)
        sel = jnp.zeros((n_blk, blk), F32)
        for j in range(n_blk):
            gj = gate[j:j + 1, :]
            beats = past & ((gate > gj) | ((gate == gj) & (jrow < j)))
            n_beat = jnp.sum(beats.astype(F32), axis=0, keepdims=True)
            sel = jnp.where(jrow == j, (n_beat < MOBA_TOPK).astype(F32), sel)
        bias_ref[hh] = jnp.where(past & (sel > 0.5), 0.0, -jnp.inf)
        qs = qh * jnp.asarray(scale, BF16)
        heads.append((qs, slope * dpos, slope))

    def scores(j):
        kb = k_ref[pl.ds(pl.multiple_of(j * blk, blk), blk), :]
        return [lax.dot_general(kb, qs, NT_DIMS, preferred_element_type=F32) - sd for qs, sd, _ in heads]

    def absorb(s, j, hh, m, l, acc):
        vt = vt_ref[j][hh * HEAD_DIM:(hh + 1) * HEAD_DIM]
        m_new = jnp.maximum(m, jnp.max(s, axis=0, keepdims=True))
        shift = jnp.where(m_new == -jnp.inf, 0.0, m_new)
        a = jnp.exp(m - shift)
        p = jnp.exp(s - shift)
        l = a * l + jnp.sum(p, axis=0, keepdims=True)
        acc = a * acc + jnp.dot(vt, p.astype(BF16), preferred_element_type=F32)
        return m_new, l, acc

    def past_block(j, carry):
        new = list(scores(j + 1))
        for hh, (_, _, slope) in enumerate(heads):
            m, l, acc = carry[2 + 3 * hh:5 + 3 * hh]
            rb = bias_ref[hh, pl.ds(j, 1), :] - slope * ((qi - j) * blk).astype(F32)
            new += list(absorb(carry[hh] + rb, j, hh, m, l, acc))
        return tuple(new)

    init = list(scores(0))
    for _ in heads:
        init += [jnp.full((1, blk), -jnp.inf, F32), jnp.zeros((1, blk), F32), jnp.zeros((HEAD_DIM, blk), F32)]
    carry = lax.fori_loop(0, qi // 2, lambda t, c: past_block(2 * t + 1, past_block(2 * t, c)), tuple(init))
    carry = lax.cond(qi % 2 == 1, lambda c: past_block(qi - 1, c), lambda c: c, carry)
    outs = []
    for hh in range(len(heads)):
        m, l, acc = carry[2 + 3 * hh:5 + 3 * hh]
        m, l, acc = absorb(jnp.where(dpos >= 0, carry[hh], -jnp.inf), qi, hh, m, l, acc)
        outs.append(acc / l)
    o_ref[...] = jnp.concatenate(outs, axis=0).T.astype(o_ref.dtype)


def _moba_attention(qkv, d):
    bsz, seq, _ = qkv.shape
    assert seq % MOBA_BLOCK == 0
    n_blk = seq // MOBA_BLOCK
    ncol = d // LANES
    return pl.pallas_call(
        functools.partial(_moba_kernel, n_blk=n_blk, scale=HEAD_DIM ** -0.5),
        grid=(bsz, N_PAIRS, n_blk),
        in_specs=[pl.BlockSpec(memory_space=pltpu.SMEM),
                  pl.BlockSpec((None, MOBA_BLOCK, LANES), lambda b, p, i: (b, i, p)),
                  pl.BlockSpec((None, seq, LANES), lambda b, p, i: (b, 0, ncol + p)),
                  pl.BlockSpec((None, seq, LANES), lambda b, p, i: (b, 0, 2 * ncol + p))],
        out_specs=pl.BlockSpec((None, MOBA_BLOCK, LANES), lambda b, p, i: (b, i, p)),
        out_shape=jax.ShapeDtypeStruct((bsz, seq, d), BF16),
        scratch_shapes=[pltpu.VMEM((n_blk, LANES), F32), pltpu.VMEM((n_blk, LANES, MOBA_BLOCK), BF16),
                        pltpu.VMEM((2, n_blk, MOBA_BLOCK), F32)],
        compiler_params=_params("parallel", "parallel", "arbitrary"),
        name="moba_attn",
    )(_alibi_slopes(), qkv, qkv, qkv)


def _swa_head_order():
    order = []
    for pair in range(N_PAIRS):
        r, i = divmod(pair, SWA_GROUP)
        order += [(2 * r) * SWA_GROUP + i, (2 * r + 1) * SWA_GROUP + i]
    return order


def _swa_kernel(slopes_ref, sinks_ref, q_ref, k_ref, v_ref, o_ref, vt_ref, *, scale, n_win):
    pair = pl.program_id(1)
    n = pl.program_id(2)
    win = SWA_WINDOW
    r = pair // SWA_GROUP
    i = pair % SWA_GROUP

    @pl.when(n == 0)
    def _():
        for j in range(n_win):
            vt_ref[j] = v_ref[j * win:(j + 1) * win, :].astype(F32).T.astype(BF16)

    w0 = n * SWA_QWINS
    q0 = pl.multiple_of(w0 * win, win)
    before0 = pl.multiple_of(jnp.maximum(w0 - 1, 0) * win, win)
    k_blocks = [k_ref[pl.ds(before0, win), :]] + [k_ref[pl.ds(q0 + w * win, win), :] for w in range(SWA_QWINS)]
    vt_blocks = [vt_ref[jnp.maximum(w0 - 1, 0)]] + [vt_ref[w0 + w] for w in range(SWA_QWINS)]
    krow = lax.broadcasted_iota(jnp.int32, (2 * win, win), 0)
    qcol = lax.broadcasted_iota(jnp.int32, (2 * win, win), 1)
    dist = (qcol + win - krow).astype(F32)
    in_window = (krow > qcol) & (krow <= qcol + win)

    per_head = []
    for hh in range(2):
        head = (2 * r + hh) * SWA_GROUP + i
        per_head.append((slopes_ref[head] * dist, sinks_ref[head]))

    for w in range(SWA_QWINS):
        q2 = q_ref[w * win:(w + 1) * win, :]
        k2 = jnp.concatenate([k_blocks[w], k_blocks[w + 1]], axis=0)
        vt2 = jnp.concatenate([vt_blocks[w], vt_blocks[w + 1]], axis=1)
        allowed = in_window if w > 0 else in_window & ((krow >= win) | (n > 0))
        outs = []
        for hh, (hmask, (bias, sink)) in enumerate(zip(_head_masks(), per_head)):
            qs = jnp.where(hmask, q2, jnp.zeros_like(q2)) * jnp.asarray(scale, BF16)
            s = lax.dot_general(k2, qs, NT_DIMS, preferred_element_type=F32) - bias
            s = jnp.where(allowed, s, -jnp.inf)
            m = jnp.maximum(jnp.max(s, axis=0, keepdims=True), sink)
            p = jnp.exp(s - m)
            l = jnp.sum(p, axis=0, keepdims=True) + jnp.exp(sink - m)
            o_t = jnp.dot(vt2[hh * HEAD_DIM:(hh + 1) * HEAD_DIM], p.astype(BF16), preferred_element_type=F32)
            outs.append(o_t / l)
        o_ref[w * win:(w + 1) * win, :] = jnp.concatenate(outs, axis=0).T.astype(o_ref.dtype)


def _swa_attention(qkv, sinks, d):
    bsz, seq, _ = qkv.shape
    qb = SWA_QWINS * SWA_WINDOW
    assert seq % qb == 0
    n_win = seq // SWA_WINDOW
    ncol = d // LANES
    kv_col = SWA_KV_HEADS * HEAD_DIM // LANES
    return pl.pallas_call(
        functools.partial(_swa_kernel, scale=HEAD_DIM ** -0.5, n_win=n_win),
        grid=(bsz, N_PAIRS, seq // qb),
        in_specs=[pl.BlockSpec(memory_space=pltpu.SMEM), pl.BlockSpec(memory_space=pltpu.SMEM),
                  pl.BlockSpec((None, qb, LANES), lambda b, p, i: (b, i, p)),
                  pl.BlockSpec((None, seq, LANES), lambda b, p, i: (b, 0, ncol + p // SWA_GROUP)),
                  pl.BlockSpec((None, seq, LANES), lambda b, p, i: (b, 0, ncol + kv_col + p // SWA_GROUP))],
        out_specs=pl.BlockSpec((None, qb, LANES), lambda b, p, i: (b, i, p)),
        out_shape=jax.ShapeDtypeStruct((bsz, seq, d), BF16),
        scratch_shapes=[pltpu.VMEM((n_win, LANES, SWA_WINDOW), BF16)],
        compiler_params=_params("parallel", "parallel", "arbitrary"),
        name="swa_attn",
    )(_alibi_slopes(), sinks.astype(F32), qkv, qkv, qkv)


def _log_sigmoid(z):
    return jnp.minimum(z, 0.0) - jnp.log(1.0 + jnp.exp(-jnp.abs(z)))


def _suffix_sums(lk, upper):
    hi = lk.astype(BF16)
    lo = (lk - hi.astype(F32)).astype(BF16)
    n = lk.shape[0]
    both = jnp.dot(jnp.concatenate([hi, lo], axis=0), upper, preferred_element_type=F32)
    return both[:n] + both[n:]


def _sb_tile(qh, kb, vb, upper, later, strict):
    z = lax.dot_general(qh, kb, NT_DIMS, preferred_element_type=F32)
    ls = _log_sigmoid(z)
    lk = ls - z
    if strict is not None:
        lk = jnp.where(strict, lk, 0.0)
    after = _suffix_sums(lk, upper)
    if later is not None:
        after = after + later
    a = jnp.exp(ls + after)
    if strict is not None:
        a = jnp.where(strict, a, 0.0)
    mass = jnp.sum(lk, axis=1, keepdims=True)
    return jnp.dot(a.astype(BF16), vb, preferred_element_type=F32), (mass if later is None else later + mass)


def _sb_kernel(q_ref, k_ref, v_ref, o_ref, *, scale):
    c = pl.program_id(2)
    blk = SB_TILE
    row = lax.broadcasted_iota(jnp.int32, (blk, blk), 0)
    col = lax.broadcasted_iota(jnp.int32, (blk, blk), 1)
    strict = col < row
    upper = (row > col).astype(BF16)
    own0 = pl.multiple_of(c * blk, blk)
    k_own = k_ref[pl.ds(own0, blk), :]
    v_own = v_ref[pl.ds(own0, blk), :]
    q2 = q_ref[...]
    masks = _head_masks()
    qhs = [jnp.where(hmask, q2, jnp.zeros_like(q2)) * jnp.asarray(scale, BF16) for hmask in masks]

    carry = []
    for qh in qhs:
        acc, later = _sb_tile(qh, k_own, v_own, upper, None, strict)
        carry += [later, acc]

    def earlier_block(state):
        step, carry = state[0], state[1:]
        j0 = pl.multiple_of((c - 1 - step) * blk, blk)
        kb = k_ref[pl.ds(j0, blk), :]
        vb = v_ref[pl.ds(j0, blk), :]
        new = [step + 1]
        for hh, qh in enumerate(qhs):
            contrib, later = _sb_tile(qh, kb, vb, upper, carry[2 * hh], None)
            new += [later, carry[2 * hh + 1] + contrib]
        return tuple(new)

    def weights_can_be_nonzero(state):
        alive = jnp.maximum(jnp.max(state[1]), jnp.max(state[3])) >= SB_DEAD_MASS
        return (state[0] < c) & alive

    carry = lax.while_loop(weights_can_be_nonzero, earlier_block, (jnp.int32(0),) + tuple(carry))[1:]
    o_ref[...] = jnp.where(masks[0], carry[1], carry[3]).astype(o_ref.dtype)


def _sb_attention(qkv, d):
    bsz, seq, _ = qkv.shape
    blk = min(SB_TILE, seq)
    assert seq % blk == 0 and blk == SB_TILE
    ncol = d // LANES
    return pl.pallas_call(
        functools.partial(_sb_kernel, scale=HEAD_DIM ** -0.5),
        grid=(bsz, N_PAIRS, seq // blk),
        in_specs=[pl.BlockSpec((None, blk, LANES), lambda b, p, i: (b, i, p)),
                  pl.BlockSpec((None, seq, LANES), lambda b, p, i: (b, 0, ncol + p)),
                  pl.BlockSpec((None, seq, LANES), lambda b, p, i: (b, 0, 2 * ncol + p))],
        out_specs=pl.BlockSpec((None, blk, LANES), lambda b, p, i: (b, i, p)),
        out_shape=jax.ShapeDtypeStruct((bsz, seq, d), BF16),
        compiler_params=_params("parallel", "parallel", "arbitrary"),
        name="sb_attn",
    )(qkv, qkv, qkv)


def _split_bf16(a):
    hi = a.astype(BF16)
    return hi, (a - hi.astype(F32)).astype(BF16)


def _router_kernel(x_ref, w_ref, b_ref, ids_ref, gates_ref):
    xh, xl = _split_bf16(x_ref[...])
    wh, wl = _split_bf16(w_ref[...])
    logits = (lax.dot_general(wh, xh, NT_DIMS, preferred_element_type=F32)
              + lax.dot_general(wh, xl, NT_DIMS, preferred_element_type=F32)
              + lax.dot_general(wl, xh, NT_DIMS, preferred_element_type=F32)) + b_ref[...]
    tm = logits.shape[1]
    gl = logits[0:N_GROUPS, :]
    g_row = lax.broadcasted_iota(jnp.int32, (N_GROUPS, tm), 0).astype(F32)
    g_max = jnp.max(gl, axis=0, keepdims=True)
    g_p = 1.0 / jnp.sum(jnp.exp(gl - g_max), axis=0, keepdims=True)
    g_idx = jnp.min(jnp.where(gl == g_max, g_row, float(N_GROUPS)), axis=0, keepdims=True)

    in_grp = jnp.zeros((EXPERTS_PER_GROUP, tm), F32)
    for g in range(N_GROUPS):
        lo = 8 + g * EXPERTS_PER_GROUP
        in_grp = jnp.where(g_idx == float(g), logits[lo:lo + EXPERTS_PER_GROUP, :], in_grp)
    e_exp = jnp.exp(in_grp - jnp.max(in_grp, axis=0, keepdims=True))
    e_p = e_exp / jnp.sum(e_exp, axis=0, keepdims=True)
    e_row = lax.broadcasted_iota(jnp.int32, (EXPERTS_PER_GROUP, tm), 0).astype(F32)
    p1 = jnp.max(e_p, axis=0, keepdims=True)
    i1 = jnp.min(jnp.where(e_p == p1, e_row, float(EXPERTS_PER_GROUP)), axis=0, keepdims=True)
    rest = jnp.where(e_row == i1, -1.0, e_p)
    p2 = jnp.max(rest, axis=0, keepdims=True)
    i2 = jnp.min(jnp.where(rest == p2, e_row, float(EXPERTS_PER_GROUP)), axis=0, keepdims=True)
    den = p1 + p2
    base = g_idx * float(EXPERTS_PER_GROUP)
    ids_ref[0:1, :] = (base + i1).astype(jnp.int32)
    ids_ref[1:2, :] = (base + i2).astype(jnp.int32)
    gates_ref[0:1, :] = g_p * (p1 / den)
    gates_ref[1:2, :] = g_p * (p2 / den)


def _router(x, w_rt, b_rt, tm=512):
    t, d = x.shape
    tm = min(tm, t)
    return pl.pallas_call(
        _router_kernel,
        grid=(t // tm,),
        in_specs=[pl.BlockSpec((tm, d), lambda i: (i, 0)),
                  pl.BlockSpec((ROUTER_ROWS, d), lambda i: (0, 0)),
                  pl.BlockSpec((ROUTER_ROWS, 1), lambda i: (0, 0))],
        out_specs=[pl.BlockSpec((2, tm), lambda i: (0, i)), pl.BlockSpec((2, tm), lambda i: (0, i))],
        out_shape=[jax.ShapeDtypeStruct((2, t), jnp.int32), jax.ShapeDtypeStruct((2, t), F32)],
        compiler_params=_params("parallel"),
        name="moe_router",
    )(x, w_rt, b_rt)


def _sc_gather_rows(table, idx):
    n = idx.shape[0]
    width = table.shape[1]
    mesh = plsc.VectorSubcoreMesh(core_axis_name="core", subcore_axis_name="subcore")

    @pl.kernel(out_type=jax.ShapeDtypeStruct((n, width), table.dtype), mesh=mesh, scratch_types=[])
    def gather(table_hbm, idx_hbm, out_hbm):
        def step(idx_vmem, out_vmem):
            pltpu.sync_copy(table_hbm.at[idx_vmem.at[0]], out_vmem)

        pltpu.emit_pipeline(
            step,
            grid=(n // SC_WINDOW,),
            in_specs=[pl.BlockSpec((1, SC_WINDOW), index_map=lambda i: (0, i))],
            out_specs=[pl.BlockSpec((SC_WINDOW, width), index_map=lambda i: (i, 0))],
            core_axis_name=("core", "subcore"),
            dimension_semantics=(pltpu.PARALLEL,),
        )(idx_hbm, out_hbm)

    return gather(table, idx.reshape(1, n))


def _gather_slabs(slabs, rows):
    n_slab, n_rows, width = slabs.shape
    idx = (jnp.arange(n_slab, dtype=jnp.int32)[:, None] * n_rows + rows[None, :]).reshape(-1)
    return _sc_gather_rows(slabs.reshape(n_slab * n_rows, width), idx).reshape(n_slab, rows.shape[0], width)


def _scatter_slabs(slabs, rows, n_out):
    n_slab, n_tok, width = slabs.shape
    n = n_slab * rows.shape[0]
    tok_blocks = n_tok // SC_WINDOW
    idx = (jnp.arange(n_slab, dtype=jnp.int32)[:, None] * n_out + rows[None, :]).reshape(1, n)
    mesh = plsc.VectorSubcoreMesh(core_axis_name="core", subcore_axis_name="subcore")

    @pl.kernel(out_type=jax.ShapeDtypeStruct((n_slab * n_out, width), slabs.dtype), mesh=mesh, scratch_types=[])
    def scatter(src_hbm, idx_hbm, out_hbm):
        def step(src_vmem, idx_vmem):
            pltpu.sync_copy(src_vmem, out_hbm.at[idx_vmem.at[0]])

        pltpu.emit_pipeline(
            step,
            grid=(n // SC_WINDOW,),
            in_specs=[pl.BlockSpec((SC_WINDOW, width),
                                   index_map=lambda i: ((i // (2 * tok_blocks)) * tok_blocks + i % tok_blocks, 0)),
                      pl.BlockSpec((1, SC_WINDOW), index_map=lambda i: (0, i))],
            out_specs=[],
            core_axis_name=("core", "subcore"),
            dimension_semantics=(pltpu.PARALLEL,),
        )(src_hbm, idx_hbm)

    return scatter(slabs.reshape(n_slab * n_tok, width), idx).reshape(n_slab, n_out, width)


def _expert_kernel(block_e_ref, nused_ref, x_ref, wg_ref, wu_ref, wd_ref, y_ref, wg_bf, wu_bf, wd_bf):
    i = pl.program_id(0)

    @pl.when(i < nused_ref[0])
    def _():
        @pl.when((i == 0) | (block_e_ref[i] != block_e_ref[jnp.maximum(i - 1, 0)]))
        def _():
            wg_bf[...] = wg_ref[...].astype(BF16)
            wu_bf[...] = wu_ref[...].astype(BF16)
            wd_bf[...] = wd_ref[...].astype(BF16)

        xb = _from_slabs(x_ref).astype(BF16)
        g = jnp.dot(xb, wg_bf[...], preferred_element_type=F32)
        u = jnp.dot(xb, wu_bf[...], preferred_element_type=F32)
        h = (g * jax.nn.sigmoid(g)) * u
        _to_slabs(y_ref, jnp.dot(h.astype(BF16), wd_bf[...], preferred_element_type=F32))


def _expert_ffn(block_e, nused, xs, wg, wu, wd, layer):
    n_slab, n_rows, width = xs.shape
    d, de = wg.shape[1], wg.shape[2]
    n_blocks = n_rows // MOE_BLOCK
    base = layer * N_EXPERTS

    def rows(i, be, nu):
        return (0, jnp.minimum(i, nu[0] - 1), 0)

    def expert(i, be, nu):
        return (base + be[i], 0, 0)

    return pl.pallas_call(
        _expert_kernel,
        grid_spec=pltpu.PrefetchScalarGridSpec(
            num_scalar_prefetch=2,
            grid=(n_blocks,),
            in_specs=[pl.BlockSpec((n_slab, MOE_BLOCK, width), rows),
                      pl.BlockSpec((None, d, de), expert),
                      pl.BlockSpec((None, d, de), expert),
                      pl.BlockSpec((None, de, d), expert)],
            out_specs=pl.BlockSpec((n_slab, MOE_BLOCK, width), rows),
            scratch_shapes=[pltpu.VMEM((d, de), BF16), pltpu.VMEM((d, de), BF16), pltpu.VMEM((de, d), BF16)]),
        out_shape=jax.ShapeDtypeStruct((n_slab, n_rows, width), F32),
        compiler_params=_params("arbitrary"),
        name="moe_experts",
    )(block_e, nused, xs, wg, wu, wd)


def _plan_kernel(e_ref, row_ref, block_e_ref, nused_ref):
    e_all = e_ref[...]
    n_chunk = e_all.shape[0]
    lane_before = (lax.broadcasted_iota(jnp.int32, (LANES, LANES), 0)
                   < lax.broadcasted_iota(jnp.int32, (LANES, LANES), 1))
    within = jnp.concatenate([lane_before.astype(BF16), jnp.ones((LANES, LANES), BF16)], axis=1)
    chunk_before = (lax.broadcasted_iota(jnp.int32, (n_chunk, n_chunk), 1)
                    < lax.broadcasted_iota(jnp.int32, (n_chunk, n_chunk), 0)).astype(BF16)
    block_lane = lax.broadcasted_iota(jnp.int32, block_e_ref.shape, 1).astype(F32)

    row = jnp.zeros(e_all.shape, F32)
    blocks_done = jnp.zeros((1, LANES), F32)
    experts_done = jnp.zeros(block_e_ref.shape, F32)
    for e in range(N_EXPERTS):
        mine = e_all == e
        counts = jnp.dot(mine.astype(BF16), within, preferred_element_type=F32)
        in_chunk, chunk_total = counts[:, :LANES], counts[:, LANES:]
        earlier_chunks = jnp.dot(chunk_before, chunk_total.astype(BF16), preferred_element_type=F32)
        n_mine = earlier_chunks[n_chunk - 1:, :] + chunk_total[n_chunk - 1:, :]
        row = jnp.where(mine, blocks_done * MOE_BLOCK + earlier_chunks + in_chunk, row)
        blocks_done = blocks_done + jnp.floor((n_mine + (MOE_BLOCK - 1)) * (1.0 / MOE_BLOCK))
        experts_done = experts_done + (block_lane >= blocks_done[:, 0:1]).astype(F32)
    row_ref[...] = row.astype(jnp.int32)
    block_e_ref[...] = jnp.minimum(experts_done, N_EXPERTS - 1).astype(jnp.int32)
    nused_ref[...] = blocks_done.astype(jnp.int32)


def _dispatch_plan(ids, n_tok):
    n_asg = 2 * n_tok
    n_blocks = n_asg // MOE_BLOCK + N_EXPERTS
    block_lanes = -(-n_blocks // LANES) * LANES
    rows, block_e, nused = pl.pallas_call(
        _plan_kernel,
        out_shape=[jax.ShapeDtypeStruct((n_asg // LANES, LANES), jnp.int32),
                   jax.ShapeDtypeStruct((1, block_lanes), jnp.int32),
                   jax.ShapeDtypeStruct((1, LANES), jnp.int32)],
        compiler_params=pltpu.CompilerParams(vmem_limit_bytes=VMEM_LIMIT),
        name="moe_plan",
    )(ids.reshape(n_asg // LANES, LANES))
    return block_e[0, :n_blocks], nused[0, :1], rows.reshape(n_asg)


def _moe_outputs(x, x_slabs, w_rt, b_rt, wg, wu, wd, layer):
    n_tok = x.shape[0]
    ids, gates = _router(x, w_rt, b_rt)
    block_e, nused, asg_row = _dispatch_plan(ids, n_tok)
    xs = _scatter_slabs(x_slabs, asg_row, block_e.shape[0] * MOE_BLOCK)
    ys = _expert_ffn(block_e, nused, xs, wg, wu, wd, layer)
    return _gather_slabs(ys, asg_row), gates


def _router_weights(w_grp, b_grp, w_exp, b_exp):
    d = w_grp.shape[0]
    pad = 8 - N_GROUPS
    w_rt = jnp.concatenate([w_grp.T, jnp.zeros((pad, d), F32), w_exp.T], axis=0)
    b_rt = jnp.concatenate([b_grp, jnp.zeros((pad,), F32), b_exp]).reshape(ROUTER_ROWS, 1)
    return w_rt.astype(F32), b_rt.astype(F32)


def kernel(x, p, w_qkv_a, w_o_a, w_qkv_b, w_o_b, sinks_b, w_qkv_c, w_o_c, ln1_g, ln1_b, ln2_g, ln2_b, w_grp, b_grp, w_exp, b_exp, w_e_gate, w_e_up, w_e_down, w_ple_gate, w_ple_proj):
    bsz, seq, d = x.shape
    depth = p.shape[0]
    n_tok = bsz * seq
    alpha = (2.0 * depth) ** 0.25
    kv_w = SWA_KV_HEADS * HEAD_DIM
    swa_cols = jnp.asarray(np.concatenate([np.arange(h * HEAD_DIM, (h + 1) * HEAD_DIM) for h in _swa_head_order()]))

    n_all = depth * N_EXPERTS
    wg_all = w_e_gate.reshape(n_all, d, -1)
    wu_all = w_e_up.reshape(n_all, d, -1)
    wd_all = w_e_down.reshape(n_all, -1, d)
    p_all = p.reshape(depth * n_tok, -1)

    xt = x.reshape(n_tok, d)
    for i in range(depth):
        mixer, j = i % 3, i // 3
        if mixer == 0:
            qkv = _proj(xt, w_qkv_a[j].astype(BF16))
            o = _moba_attention(qkv.reshape(bsz, seq, 3 * d), d)
            w_o = w_o_a[j]
        elif mixer == 1:
            w_qkv = jnp.concatenate([w_qkv_b[j][:, :d][:, swa_cols], w_qkv_b[j][:, d:]], axis=1)
            qkv = _proj(xt, w_qkv.astype(BF16))
            o = _swa_attention(qkv.reshape(bsz, seq, d + 2 * kv_w), sinks_b[j], d)
            w_o = w_o_b[j][swa_cols, :]
        else:
            qkv = _proj(xt, w_qkv_c[j].astype(BF16))
            o = _sb_attention(qkv.reshape(bsz, seq, 3 * d), d)
            w_o = w_o_c[j]
        x1, x1_slabs = _oproj_ln(o.reshape(n_tok, d), w_o.astype(BF16), xt, ln1_g[i], ln1_b[i], alpha)

        w_rt, b_rt = _router_weights(w_grp[i], b_grp[i], w_exp[i], b_exp[i])
        m2, gates = _moe_outputs(x1, x1_slabs, w_rt, b_rt, wg_all, wu_all, wd_all, i)
        w0 = jnp.broadcast_to(gates[0][:, None], (n_tok, LANES))
        w1 = jnp.broadcast_to(gates[1][:, None], (n_tok, LANES))
        xt = _ln_ple(x1, m2, w0, w1, ln2_g[i], ln2_b[i], p_all, i,
                     w_ple_gate[i].astype(BF16), w_ple_proj[i].astype(BF16), alpha)
    return xt.reshape(bsz, seq, d)
```

```python
import functools

import jax
import jax.numpy as jnp
import numpy as np
from jax import lax
from jax.experimental import pallas as pl
from jax.experimental.pallas import tpu as pltpu
from jax.experimental.pallas import tpu_sc as plsc

F32 = jnp.float32
BF16 = jnp.bfloat16

LANES = 128
HEAD_DIM = 64
N_HEADS = 16
N_PAIRS = N_HEADS // 2
MOBA_BLOCK = 256
MOBA_TOPK = 3
SWA_WINDOW = 128
SWA_KV_HEADS = 4
SWA_GROUP = N_HEADS // SWA_KV_HEADS
SWA_QWINS = 4
SB_TILE = 512
SB_DEAD_MASS = -100.0
N_GROUPS = 4
EXPERTS_PER_GROUP = 8
N_EXPERTS = N_GROUPS * EXPERTS_PER_GROUP
MOE_BLOCK = 512
LN_EPS = 1e-5
ROUTER_ROWS = 8 + N_EXPERTS
N_SLABS = 4
PACKED_SLABS = 2
SC_WINDOW = 128
VMEM_LIMIT = 48 * 1024 * 1024

NT_DIMS = (((1,), (1,)), ((), ()))


def _alibi_slopes():
    return jnp.asarray(np.array([2.0 ** (-8.0 * (h + 1) / N_HEADS) for h in range(N_HEADS)], dtype=np.float32))


def _params(*semantics):
    return pltpu.CompilerParams(dimension_semantics=semantics, vmem_limit_bytes=VMEM_LIMIT)


def _head_masks():
    lane = lax.broadcasted_iota(jnp.int32, (1, LANES), 1)
    return (lane < HEAD_DIM, lane >= HEAD_DIM)


def _proj_kernel(x_ref, w_ref, o_ref):
    o_ref[...] = jnp.dot(x_ref[...].astype(BF16), w_ref[...], preferred_element_type=F32).astype(o_ref.dtype)


def _proj(x, w, tm=1024, tn=512):
    t, k = x.shape
    n = w.shape[1]
    tm = min(tm, t)
    return pl.pallas_call(
        _proj_kernel,
        grid=(t // tm, n // tn),
        in_specs=[pl.BlockSpec((tm, k), lambda i, j: (i, 0)), pl.BlockSpec((k, tn), lambda i, j: (0, j))],
        out_specs=pl.BlockSpec((tm, tn), lambda i, j: (i, j)),
        out_shape=jax.ShapeDtypeStruct((t, n), BF16),
        compiler_params=_params("parallel", "parallel"),
        name="qkv_proj",
    )(x, w)


def _layer_norm(y, g, b):
    mu = jnp.mean(y, axis=-1, keepdims=True)
    d = y - mu
    var = jnp.mean(d * d, axis=-1, keepdims=True)
    return d * lax.rsqrt(var + LN_EPS) * g + b


def _to_slabs(slab_ref, y):
    w = y.shape[1] // N_SLABS
    for c in range(N_SLABS):
        slab_ref[c] = y[:, c * w:(c + 1) * w]


def _from_slabs(slab_ref):
    return jnp.concatenate([slab_ref[c] for c in range(N_SLABS)], axis=1)


def _to_packed_slabs(slab_ref, y):
    half = y.shape[1] // 2
    bits = lax.bitcast_convert_type(y.astype(BF16).astype(F32), jnp.uint32)
    words = bits[:, :half] | (bits[:, half:] >> 16)
    w = half // PACKED_SLABS
    for c in range(PACKED_SLABS):
        slab_ref[c] = words[:, c * w:(c + 1) * w]


def _from_packed_slabs(slab_ref):
    words = jnp.concatenate([slab_ref[c] for c in range(PACKED_SLABS)], axis=1)
    first = lax.bitcast_convert_type(words & jnp.uint32(0xFFFF0000), F32)
    second = lax.bitcast_convert_type(words << 16, F32)
    return jnp.concatenate([first, second], axis=1).astype(BF16)


def _oproj_ln_kernel(o_ref, w_ref, x_ref, g_ref, b_ref, wr_ref, br_ref, out_ref, slab_ref, ids_ref, gates_ref, *, alpha):
    h = jnp.dot(o_ref[...], w_ref[...], preferred_element_type=F32)
    y = _layer_norm(alpha * x_ref[...] + h, g_ref[...], b_ref[...])
    out_ref[...] = y
    _to_packed_slabs(slab_ref, y)
    _route(y, wr_ref[...], br_ref[...], ids_ref, gates_ref)


def _oproj_ln(o, w, x, g, b, w_rt, b_rt, alpha, tm=512):
    t, d = x.shape
    tm = min(tm, t)
    row = pl.BlockSpec((tm, d), lambda i: (i, 0))
    vec = pl.BlockSpec((1, d), lambda i: (0, 0))
    per_tok = pl.BlockSpec((2, tm), lambda i: (0, i))
    return pl.pallas_call(
        functools.partial(_oproj_ln_kernel, alpha=alpha),
        grid=(t // tm,),
        in_specs=[row, pl.BlockSpec((d, d), lambda i: (0, 0)), row, vec, vec,
                  pl.BlockSpec((ROUTER_ROWS, d), lambda i: (0, 0)), pl.BlockSpec((ROUTER_ROWS, 1), lambda i: (0, 0))],
        out_specs=[row, pl.BlockSpec((PACKED_SLABS, tm, d // 2 // PACKED_SLABS), lambda i: (0, i, 0)),
                   per_tok, per_tok],
        out_shape=[jax.ShapeDtypeStruct((t, d), F32),
                   jax.ShapeDtypeStruct((PACKED_SLABS, t, d // 2 // PACKED_SLABS), jnp.uint32),
                   jax.ShapeDtypeStruct((2, t), jnp.int32), jax.ShapeDtypeStruct((2, t), F32)],
        compiler_params=_params("parallel"),
        name="oproj_ln",
    )(o, w, x, g.reshape(1, d), b.reshape(1, d), w_rt, b_rt)


def _ln_ple_kernel(x_ref, m0_ref, m1_ref, w0_ref, w1_ref, g_ref, b_ref, p_ref, wg_ref, wp_ref, out_ref, *, alpha):
    m = _from_slabs(m0_ref) * w0_ref[:, 0:1] + _from_slabs(m1_ref) * w1_ref[:, 0:1]
    x2 = _layer_norm(alpha * x_ref[...] + m, g_ref[...], b_ref[...])
    gate = jax.nn.sigmoid(jnp.dot(x2.astype(BF16), wg_ref[...], preferred_element_type=F32))
    proj = jnp.dot(p_ref[...].astype(BF16), wp_ref[...], preferred_element_type=F32)
    out_ref[...] = x2 + gate * proj


def _ln_ple(x, m2, w0, w1, g, b, p, layer, wg, wp, alpha, tm=512):
    t, d = x.shape
    pd = p.shape[1]
    tm = min(tm, t)
    nt = t // tm
    p0 = layer * nt
    row = pl.BlockSpec((tm, d), lambda i: (i, 0))
    rep = pl.BlockSpec((tm, LANES), lambda i: (i, 0))
    vec = pl.BlockSpec((1, d), lambda i: (0, 0))
    return pl.pallas_call(
        functools.partial(_ln_ple_kernel, alpha=alpha),
        grid=(nt,),
        in_specs=[row, pl.BlockSpec((N_SLABS, tm, d // N_SLABS), lambda i: (0, i, 0)),
                  pl.BlockSpec((N_SLABS, tm, d // N_SLABS), lambda i: (0, i + nt, 0)), rep, rep, vec, vec,
                  pl.BlockSpec((tm, pd), lambda i: (p0 + i, 0)),
                  pl.BlockSpec((d, d), lambda i: (0, 0)), pl.BlockSpec((pd, d), lambda i: (0, 0))],
        out_specs=row,
        out_shape=jax.ShapeDtypeStruct((t, d), F32),
        compiler_params=_params("parallel"),
        name="ln_ple",
    )(x, m2, m2, w0, w1, g.reshape(1, d), b.reshape(1, d), p, wg, wp)


def _moba_kernel(slopes_ref, q_ref, k_ref, v_ref, o_ref, kmean_ref, vt_ref, bias_ref, *, n_blk, scale):
    pair = pl.program_id(1)
    qi = pl.program_id(2)
    blk = MOBA_BLOCK

    @pl.when(qi == 0)
    def _():
        for j in range(n_blk):
            kb = k_ref[j * blk:(j + 1) * blk, :].astype(F32)
            kmean_ref[j:j + 1, :] = jnp.sum(kb, axis=0, keepdims=True) * (1.0 / blk)
            vt_ref[j] = v_ref[j * blk:(j + 1) * blk, :].astype(F32).T.astype(BF16)

    kmean = kmean_ref[...].astype(BF16)
    krow = lax.broadcasted_iota(jnp.int32, (blk, blk), 0)
    qcol = lax.broadcasted_iota(jnp.int32, (blk, blk), 1)
    dpos = (qcol - krow).astype(F32)
    jrow = lax.broadcasted_iota(jnp.int32, (n_blk, blk), 0)
    past = jrow < qi
    q2 = q_ref[...]

    heads = []
    for hh, hmask in enumerate(_head_masks()):
        slope = slopes_ref[2 * pair + hh]
        qh = jnp.where(hmask, q2, jnp.zeros_like(q2))

        gate = lax.dot_general(kmean, qh, NT_DIMS, preferred_element_type=F32)
        sel = jnp.zeros((n_blk, blk), F32)
        for j in range(n_blk):
            gj = gate[j:j + 1, :]
            beats = past & ((gate > gj) | ((gate == gj) & (jrow < j)))
            n_beat = jnp.sum(beats.astype(F32), axis=0, keepdims=True)
            sel = jnp.where(jrow == j, (n_beat < MOBA_TOPK).astype(F32), sel)
        bias_ref[hh] = jnp.where(past & (sel > 0.5), 0.0, -jnp.inf)
        qs = qh * jnp.asarray(scale, BF16)
        heads.append((qs, slope * dpos, slope))

    def scores(j):
        kb = k_ref[pl.ds(pl.multiple_of(j * blk, blk), blk), :]
        return [lax.dot_general(kb, qs, NT_DIMS, preferred_element_type=F32) - sd for qs, sd, _ in heads]

    def absorb(s, j, hh, m, l, acc):
        vt = vt_ref[j][hh * HEAD_DIM:(hh + 1) * HEAD_DIM]
        m_new = jnp.maximum(m, jnp.max(s, axis=0, keepdims=True))
        shift = jnp.where(m_new == -jnp.inf, 0.0, m_new)
        a = jnp.exp(m - shift)
        p = jnp.exp(s - shift)
        l = a * l + jnp.sum(p, axis=0, keepdims=True)
        acc = a * acc + jnp.dot(vt, p.astype(BF16), preferred_element_type=F32)
        return m_new, l, acc

    def past_block(j, carry):
        new = list(scores(j + 1))
        for hh, (_, _, slope) in enumerate(heads):
            m, l, acc = carry[2 + 3 * hh:5 + 3 * hh]
            rb = bias_ref[hh, pl.ds(j, 1), :] - slope * ((qi - j) * blk).astype(F32)
            new += list(absorb(carry[hh] + rb, j, hh, m, l, acc))
        return tuple(new)

    init = list(scores(0))
    for _ in heads:
        init += [jnp.full((1, blk), -jnp.inf, F32), jnp.zeros((1, blk), F32), jnp.zeros((HEAD_DIM, blk), F32)]
    carry = lax.fori_loop(0, qi // 2, lambda t, c: past_block(2 * t + 1, past_block(2 * t, c)), tuple(init))
    carry = lax.cond(qi % 2 == 1, lambda c: past_block(qi - 1, c), lambda c: c, carry)
    outs = []
    for hh in range(len(heads)):
        m, l, acc = carry[2 + 3 * hh:5 + 3 * hh]
        m, l, acc = absorb(jnp.where(dpos >= 0, carry[hh], -jnp.inf), qi, hh, m, l, acc)
        outs.append(acc / l)
    o_ref[...] = jnp.concatenate(outs, axis=0).T.astype(o_ref.dtype)


def _moba_attention(qkv, d):
    bsz, seq, _ = qkv.shape
    assert seq % MOBA_BLOCK == 0
    n_blk = seq // MOBA_BLOCK
    ncol = d // LANES
    return pl.pallas_call(
        functools.partial(_moba_kernel, n_blk=n_blk, scale=HEAD_DIM ** -0.5),
        grid=(bsz, N_PAIRS, n_blk),
        in_specs=[pl.BlockSpec(memory_space=pltpu.SMEM),
                  pl.BlockSpec((None, MOBA_BLOCK, LANES), lambda b, p, i: (b, i, p)),
                  pl.BlockSpec((None, seq, LANES), lambda b, p, i: (b, 0, ncol + p)),
                  pl.BlockSpec((None, seq, LANES), lambda b, p, i: (b, 0, 2 * ncol + p))],
        out_specs=pl.BlockSpec((None, MOBA_BLOCK, LANES), lambda b, p, i: (b, i, p)),
        out_shape=jax.ShapeDtypeStruct((bsz, seq, d), BF16),
        scratch_shapes=[pltpu.VMEM((n_blk, LANES), F32), pltpu.VMEM((n_blk, LANES, MOBA_BLOCK), BF16),
                        pltpu.VMEM((2, n_blk, MOBA_BLOCK), F32)],
        compiler_params=_params("parallel", "parallel", "arbitrary"),
        name="moba_attn",
    )(_alibi_slopes(), qkv, qkv, qkv)


def _swa_head_order():
    order = []
    for pair in range(N_PAIRS):
        r, i = divmod(pair, SWA_GROUP)
        order += [(2 * r) * SWA_GROUP + i, (2 * r + 1) * SWA_GROUP + i]
    return order


def _swa_kernel(slopes_ref, sinks_ref, q_ref, k_ref, v_ref, o_ref, vt_ref, *, scale, n_win):
    pair = pl.program_id(1)
    n = pl.program_id(2)
    win = SWA_WINDOW
    r = pair // SWA_GROUP
    i = pair % SWA_GROUP

    @pl.when(n == 0)
    def _():
        for j in range(n_win):
            vt_ref[j] = v_ref[j * win:(j + 1) * win, :].astype(F32).T.astype(BF16)

    w0 = n * SWA_QWINS
    q0 = pl.multiple_of(w0 * win, win)
    before0 = pl.multiple_of(jnp.maximum(w0 - 1, 0) * win, win)
    k_blocks = [k_ref[pl.ds(before0, win), :]] + [k_ref[pl.ds(q0 + w * win, win), :] for w in range(SWA_QWINS)]
    vt_blocks = [vt_ref[jnp.maximum(w0 - 1, 0)]] + [vt_ref[w0 + w] for w in range(SWA_QWINS)]
    krow = lax.broadcasted_iota(jnp.int32, (2 * win, win), 0)
    qcol = lax.broadcasted_iota(jnp.int32, (2 * win, win), 1)
    dist = (qcol + win - krow).astype(F32)
    in_window = (krow > qcol) & (krow <= qcol + win)

    per_head = []
    for hh in range(2):
        head = (2 * r + hh) * SWA_GROUP + i
        per_head.append((slopes_ref[head] * dist, sinks_ref[head]))

    for w in range(SWA_QWINS):
        q2 = q_ref[w * win:(w + 1) * win, :]
        k2 = jnp.concatenate([k_blocks[w], k_blocks[w + 1]], axis=0)
        vt2 = jnp.concatenate([vt_blocks[w], vt_blocks[w + 1]], axis=1)
        allowed = in_window if w > 0 else in_window & ((krow >= win) | (n > 0))
        outs = []
        for hh, (hmask, (bias, sink)) in enumerate(zip(_head_masks(), per_head)):
            qs = jnp.where(hmask, q2, jnp.zeros_like(q2)) * jnp.asarray(scale, BF16)
            s = lax.dot_general(k2, qs, NT_DIMS, preferred_element_type=F32) - bias
            s = jnp.where(allowed, s, -jnp.inf)
            m = jnp.maximum(jnp.max(s, axis=0, keepdims=True), sink)
            p = jnp.exp(s - m)
            l = jnp.sum(p, axis=0, keepdims=True) + jnp.exp(sink - m)
            o_t = jnp.dot(vt2[hh * HEAD_DIM:(hh + 1) * HEAD_DIM], p.astype(BF16), preferred_element_type=F32)
            outs.append(o_t / l)
        o_ref[w * win:(w + 1) * win, :] = jnp.concatenate(outs, axis=0).T.astype(o_ref.dtype)


def _swa_attention(qkv, sinks, d):
    bsz, seq, _ = qkv.shape
    qb = SWA_QWINS * SWA_WINDOW
    assert seq % qb == 0
    n_win = seq // SWA_WINDOW
    ncol = d // LANES
    kv_col = SWA_KV_HEADS * HEAD_DIM // LANES
    return pl.pallas_call(
        functools.partial(_swa_kernel, scale=HEAD_DIM ** -0.5, n_win=n_win),
        grid=(bsz, N_PAIRS, seq // qb),
        in_specs=[pl.BlockSpec(memory_space=pltpu.SMEM), pl.BlockSpec(memory_space=pltpu.SMEM),
                  pl.BlockSpec((None, qb, LANES), lambda b, p, i: (b, i, p)),
                  pl.BlockSpec((None, seq, LANES), lambda b, p, i: (b, 0, ncol + p // SWA_GROUP)),
                  pl.BlockSpec((None, seq, LANES), lambda b, p, i: (b, 0, ncol + kv_col + p // SWA_GROUP))],
        out_specs=pl.BlockSpec((None, qb, LANES), lambda b, p, i: (b, i, p)),
        out_shape=jax.ShapeDtypeStruct((bsz, seq, d), BF16),
        scratch_shapes=[pltpu.VMEM((n_win, LANES, SWA_WINDOW), BF16)],
        compiler_params=_params("parallel", "parallel", "arbitrary"),
        name="swa_attn",
    )(_alibi_slopes(), sinks.astype(F32), qkv, qkv, qkv)


def _log_sigmoid(z):
    return jnp.minimum(z, 0.0) - jnp.log(1.0 + jnp.exp(-jnp.abs(z)))


def _suffix_sums(lk, upper):
    hi = lk.astype(BF16)
    lo = (lk - hi.astype(F32)).astype(BF16)
    n = lk.shape[0]
    both = jnp.dot(jnp.concatenate([hi, lo], axis=0), upper, preferred_element_type=F32)
    return both[:n] + both[n:]


def _sb_tile(qh, kb, vb, upper, later, strict):
    z = lax.dot_general(qh, kb, NT_DIMS, preferred_element_type=F32)
    ls = _log_sigmoid(z)
    lk = ls - z
    if strict is not None:
        lk = jnp.where(strict, lk, 0.0)
    after = _suffix_sums(lk, upper)
    if later is not None:
        after = after + later
    a = jnp.exp(ls + after)
    if strict is not None:
        a = jnp.where(strict, a, 0.0)
    mass = jnp.sum(lk, axis=1, keepdims=True)
    return jnp.dot(a.astype(BF16), vb, preferred_element_type=F32), (mass if later is None else later + mass)


def _sb_kernel(q_ref, k_ref, v_ref, o_ref, *, scale):
    c = pl.program_id(2)
    blk = SB_TILE
    row = lax.broadcasted_iota(jnp.int32, (blk, blk), 0)
    col = lax.broadcasted_iota(jnp.int32, (blk, blk), 1)
    strict = col < row
    upper = (row > col).astype(BF16)
    own0 = pl.multiple_of(c * blk, blk)
    k_own = k_ref[pl.ds(own0, blk), :]
    v_own = v_ref[pl.ds(own0, blk), :]
    q2 = q_ref[...]
    masks = _head_masks()
    qhs = [jnp.where(hmask, q2, jnp.zeros_like(q2)) * jnp.asarray(scale, BF16) for hmask in masks]

    carry = []
    for qh in qhs:
        acc, later = _sb_tile(qh, k_own, v_own, upper, None, strict)
        carry += [later, acc]

    def earlier_block(state):
        step, carry = state[0], state[1:]
        j0 = pl.multiple_of((c - 1 - step) * blk, blk)
        kb = k_ref[pl.ds(j0, blk), :]
        vb = v_ref[pl.ds(j0, blk), :]
        new = [step + 1]
        for hh, qh in enumerate(qhs):
            contrib, later = _sb_tile(qh, kb, vb, upper, carry[2 * hh], None)
            new += [later, carry[2 * hh + 1] + contrib]
        return tuple(new)

    def weights_can_be_nonzero(state):
        alive = jnp.maximum(jnp.max(state[1]), jnp.max(state[3])) >= SB_DEAD_MASS
        return (state[0] < c) & alive

    carry = lax.while_loop(weights_can_be_nonzero, earlier_block, (jnp.int32(0),) + tuple(carry))[1:]
    o_ref[...] = jnp.where(masks[0], carry[1], carry[3]).astype(o_ref.dtype)


def _sb_attention(qkv, d):
    bsz, seq, _ = qkv.shape
    blk = min(SB_TILE, seq)
    assert seq % blk == 0 and blk == SB_TILE
    ncol = d // LANES
    return pl.pallas_call(
        functools.partial(_sb_kernel, scale=HEAD_DIM ** -0.5),
        grid=(bsz, N_PAIRS, seq // blk),
        in_specs=[pl.BlockSpec((None, blk, LANES), lambda b, p, i: (b, i, p)),
                  pl.BlockSpec((None, seq, LANES), lambda b, p, i: (b, 0, ncol + p)),
                  pl.BlockSpec((None, seq, LANES), lambda b, p, i: (b, 0, 2 * ncol + p))],
        out_specs=pl.BlockSpec((None, blk, LANES), lambda b, p, i: (b, i, p)),
        out_shape=jax.ShapeDtypeStruct((bsz, seq, d), BF16),
        compiler_params=_params("parallel", "parallel", "arbitrary"),
        name="sb_attn",
    )(qkv, qkv, qkv)


def _split_bf16(a):
    hi = a.astype(BF16)
    return hi, (a - hi.astype(F32)).astype(BF16)


def _route(x, w, b, ids_ref, gates_ref):
    xh, xl = _split_bf16(x)
    wh, wl = _split_bf16(w)
    logits = (lax.dot_general(wh, xh, NT_DIMS, preferred_element_type=F32)
              + lax.dot_general(wh, xl, NT_DIMS, preferred_element_type=F32)
              + lax.dot_general(wl, xh, NT_DIMS, preferred_element_type=F32)) + b
    tm = logits.shape[1]
    gl = logits[0:N_GROUPS, :]
    g_row = lax.broadcasted_iota(jnp.int32, (N_GROUPS, tm), 0).astype(F32)
    g_max = jnp.max(gl, axis=0, keepdims=True)
    g_p = 1.0 / jnp.sum(jnp.exp(gl - g_max), axis=0, keepdims=True)
    g_idx = jnp.min(jnp.where(gl == g_max, g_row, float(N_GROUPS)), axis=0, keepdims=True)

    in_grp = jnp.zeros((EXPERTS_PER_GROUP, tm), F32)
    for g in range(N_GROUPS):
        lo = 8 + g * EXPERTS_PER_GROUP
        in_grp = jnp.where(g_idx == float(g), logits[lo:lo + EXPERTS_PER_GROUP, :], in_grp)
    e_exp = jnp.exp(in_grp - jnp.max(in_grp, axis=0, keepdims=True))
    e_p = e_exp / jnp.sum(e_exp, axis=0, keepdims=True)
    e_row = lax.broadcasted_iota(jnp.int32, (EXPERTS_PER_GROUP, tm), 0).astype(F32)
    p1 = jnp.max(e_p, axis=0, keepdims=True)
    i1 = jnp.min(jnp.where(e_p == p1, e_row, float(EXPERTS_PER_GROUP)), axis=0, keepdims=True)
    rest = jnp.where(e_row == i1, -1.0, e_p)
    p2 = jnp.max(rest, axis=0, keepdims=True)
    i2 = jnp.min(jnp.where(rest == p2, e_row, float(EXPERTS_PER_GROUP)), axis=0, keepdims=True)
    den = p1 + p2
    base = g_idx * float(EXPERTS_PER_GROUP)
    ids_ref[0:1, :] = (base + i1).astype(jnp.int32)
    ids_ref[1:2, :] = (base + i2).astype(jnp.int32)
    gates_ref[0:1, :] = g_p * (p1 / den)
    gates_ref[1:2, :] = g_p * (p2 / den)


def _sc_gather_rows(table, idx):
    n = idx.shape[0]
    width = table.shape[1]
    mesh = plsc.VectorSubcoreMesh(core_axis_name="core", subcore_axis_name="subcore")

    @pl.kernel(out_type=jax.ShapeDtypeStruct((n, width), table.dtype), mesh=mesh, scratch_types=[])
    def gather(table_hbm, idx_hbm, out_hbm):
        def step(idx_vmem, out_vmem):
            pltpu.sync_copy(table_hbm.at[idx_vmem.at[0]], out_vmem)

        pltpu.emit_pipeline(
            step,
            grid=(n // SC_WINDOW,),
            in_specs=[pl.BlockSpec((1, SC_WINDOW), index_map=lambda i: (0, i))],
            out_specs=[pl.BlockSpec((SC_WINDOW, width), index_map=lambda i: (i, 0))],
            core_axis_name=("core", "subcore"),
            dimension_semantics=(pltpu.PARALLEL,),
        )(idx_hbm, out_hbm)

    return gather(table, idx.reshape(1, n))


def _gather_slabs(slabs, rows):
    n_slab, n_rows, width = slabs.shape
    idx = (jnp.arange(n_slab, dtype=jnp.int32)[:, None] * n_rows + rows[None, :]).reshape(-1)
    return _sc_gather_rows(slabs.reshape(n_slab * n_rows, width), idx).reshape(n_slab, rows.shape[0], width)


def _scatter_slabs(slabs, rows, n_out):
    n_slab, n_tok, width = slabs.shape
    n = n_slab * rows.shape[0]
    tok_blocks = n_tok // SC_WINDOW
    idx = (jnp.arange(n_slab, dtype=jnp.int32)[:, None] * n_out + rows[None, :]).reshape(1, n)
    mesh = plsc.VectorSubcoreMesh(core_axis_name="core", subcore_axis_name="subcore")

    @pl.kernel(out_type=jax.ShapeDtypeStruct((n_slab * n_out, width), slabs.dtype), mesh=mesh, scratch_types=[])
    def scatter(src_hbm, idx_hbm, out_hbm):
        def step(src_vmem, idx_vmem):
            pltpu.sync_copy(src_vmem, out_hbm.at[idx_vmem.at[0]])

        pltpu.emit_pipeline(
            step,
            grid=(n // SC_WINDOW,),
            in_specs=[pl.BlockSpec((SC_WINDOW, width),
                                   index_map=lambda i: ((i // (2 * tok_blocks)) * tok_blocks + i % tok_blocks, 0)),
                      pl.BlockSpec((1, SC_WINDOW), index_map=lambda i: (0, i))],
            out_specs=[],
            core_axis_name=("core", "subcore"),
            dimension_semantics=(pltpu.PARALLEL,),
        )(src_hbm, idx_hbm)

    return scatter(slabs.reshape(n_slab * n_tok, width), idx).reshape(n_slab, n_out, width)


def _expert_kernel(block_e_ref, nused_ref, x_ref, wg_ref, wu_ref, wd_ref, y_ref, wg_bf, wu_bf, wd_bf):
    i = pl.program_id(0)

    @pl.when(i < nused_ref[0])
    def _():
        @pl.when((i == 0) | (block_e_ref[i] != block_e_ref[jnp.maximum(i - 1, 0)]))
        def _():
            wg_bf[...] = wg_ref[...].astype(BF16)
            wu_bf[...] = wu_ref[...].astype(BF16)
            wd_bf[...] = wd_ref[...].astype(BF16)

        xb = _from_packed_slabs(x_ref)
        g = jnp.dot(xb, wg_bf[...], preferred_element_type=F32)
        u = jnp.dot(xb, wu_bf[...], preferred_element_type=F32)
        h = (g * jax.nn.sigmoid(g)) * u
        _to_slabs(y_ref, jnp.dot(h.astype(BF16), wd_bf[...], preferred_element_type=F32))


def _expert_ffn(block_e, nused, xs, wg, wu, wd, layer):
    n_packed, n_rows, width = xs.shape
    n_slab = N_SLABS
    d, de = wg.shape[1], wg.shape[2]
    n_blocks = n_rows // MOE_BLOCK
    base = layer * N_EXPERTS

    def rows(i, be, nu):
        return (0, jnp.minimum(i, nu[0] - 1), 0)

    def expert(i, be, nu):
        return (base + be[i], 0, 0)

    return pl.pallas_call(
        _expert_kernel,
        grid_spec=pltpu.PrefetchScalarGridSpec(
            num_scalar_prefetch=2,
            grid=(n_blocks,),
            in_specs=[pl.BlockSpec((n_packed, MOE_BLOCK, width), rows),
                      pl.BlockSpec((None, d, de), expert),
                      pl.BlockSpec((None, d, de), expert),
                      pl.BlockSpec((None, de, d), expert)],
            out_specs=pl.BlockSpec((n_slab, MOE_BLOCK, width), rows),
            scratch_shapes=[pltpu.VMEM((d, de), BF16), pltpu.VMEM((d, de), BF16), pltpu.VMEM((de, d), BF16)]),
        out_shape=jax.ShapeDtypeStruct((n_slab, n_rows, width), F32),
        compiler_params=_params("arbitrary"),
        name="moe_experts",
    )(block_e, nused, xs, wg, wu, wd)


def _plan_kernel(e_ref, row_ref, block_e_ref, nused_ref):
    e_all = e_ref[...]
    n_chunk = e_all.shape[0]
    lane_before = (lax.broadcasted_iota(jnp.int32, (LANES, LANES), 0)
                   < lax.broadcasted_iota(jnp.int32, (LANES, LANES), 1))
    within = jnp.concatenate([lane_before.astype(BF16), jnp.ones((LANES, LANES), BF16)], axis=1)
    chunk_before = (lax.broadcasted_iota(jnp.int32, (n_chunk, n_chunk), 1)
                    < lax.broadcasted_iota(jnp.int32, (n_chunk, n_chunk), 0)).astype(BF16)
    block_lane = lax.broadcasted_iota(jnp.int32, block_e_ref.shape, 1).astype(F32)

    row = jnp.zeros(e_all.shape, F32)
    blocks_done = jnp.zeros((1, LANES), F32)
    experts_done = jnp.zeros(block_e_ref.shape, F32)
    for e in range(N_EXPERTS):
        mine = e_all == e
        counts = jnp.dot(mine.astype(BF16), within, preferred_element_type=F32)
        in_chunk, chunk_total = counts[:, :LANES], counts[:, LANES:]
        earlier_chunks = jnp.dot(chunk_before, chunk_total.astype(BF16), preferred_element_type=F32)
        n_mine = earlier_chunks[n_chunk - 1:, :] + chunk_total[n_chunk - 1:, :]
        row = jnp.where(mine, blocks_done * MOE_BLOCK + earlier_chunks + in_chunk, row)
        blocks_done = blocks_done + jnp.floor((n_mine + (MOE_BLOCK - 1)) * (1.0 / MOE_BLOCK))
        experts_done = experts_done + (block_lane >= blocks_done[:, 0:1]).astype(F32)
    row_ref[...] = row.astype(jnp.int32)
    block_e_ref[...] = jnp.minimum(experts_done, N_EXPERTS - 1).astype(jnp.int32)
    nused_ref[...] = blocks_done.astype(jnp.int32)


def _dispatch_plan(ids, n_tok):
    n_asg = 2 * n_tok
    n_blocks = n_asg // MOE_BLOCK + N_EXPERTS
    block_lanes = -(-n_blocks // LANES) * LANES
    rows, block_e, nused = pl.pallas_call(
        _plan_kernel,
        out_shape=[jax.ShapeDtypeStruct((n_asg // LANES, LANES), jnp.int32),
                   jax.ShapeDtypeStruct((1, block_lanes), jnp.int32),
                   jax.ShapeDtypeStruct((1, LANES), jnp.int32)],
        compiler_params=pltpu.CompilerParams(vmem_limit_bytes=VMEM_LIMIT),
        name="moe_plan",
    )(ids.reshape(n_asg // LANES, LANES))
    return block_e[0, :n_blocks], nused[0, :1], rows.reshape(n_asg)


def _moe_outputs(x_slabs, ids, wg, wu, wd, layer):
    block_e, nused, asg_row = _dispatch_plan(ids, ids.shape[1])
    xs = _scatter_slabs(x_slabs, asg_row, block_e.shape[0] * MOE_BLOCK)
    ys = _expert_ffn(block_e, nused, xs, wg, wu, wd, layer)
    return _gather_slabs(ys, asg_row)


def _router_weights(w_grp, b_grp, w_exp, b_exp):
    d = w_grp.shape[0]
    pad = 8 - N_GROUPS
    w_rt = jnp.concatenate([w_grp.T, jnp.zeros((pad, d), F32), w_exp.T], axis=0)
    b_rt = jnp.concatenate([b_grp, jnp.zeros((pad,), F32), b_exp]).reshape(ROUTER_ROWS, 1)
    return w_rt.astype(F32), b_rt.astype(F32)


def kernel(x, p, w_qkv_a, w_o_a, w_qkv_b, w_o_b, sinks_b, w_qkv_c, w_o_c, ln1_g, ln1_b, ln2_g, ln2_b, w_grp, b_grp, w_exp, b_exp, w_e_gate, w_e_up, w_e_down, w_ple_gate, w_ple_proj):
    bsz, seq, d = x.shape
    depth = p.shape[0]
    n_tok = bsz * seq
    alpha = (2.0 * depth) ** 0.25
    kv_w = SWA_KV_HEADS * HEAD_DIM
    swa_cols = jnp.asarray(np.concatenate([np.arange(h * HEAD_DIM, (h + 1) * HEAD_DIM) for h in _swa_head_order()]))

    n_all = depth * N_EXPERTS
    wg_all = w_e_gate.reshape(n_all, d, -1)
    wu_all = w_e_up.reshape(n_all, d, -1)
    wd_all = w_e_down.reshape(n_all, -1, d)
    p_all = p.reshape(depth * n_tok, -1)

    xt = x.reshape(n_tok, d)
    for i in range(depth):
        mixer, j = i % 3, i // 3
        if mixer == 0:
            qkv = _proj(xt, w_qkv_a[j].astype(BF16))
            o = _moba_attention(qkv.reshape(bsz, seq, 3 * d), d)
            w_o = w_o_a[j]
        elif mixer == 1:
            w_qkv = jnp.concatenate([w_qkv_b[j][:, :d][:, swa_cols], w_qkv_b[j][:, d:]], axis=1)
            qkv = _proj(xt, w_qkv.astype(BF16))
            o = _swa_attention(qkv.reshape(bsz, seq, d + 2 * kv_w), sinks_b[j], d)
            w_o = w_o_b[j][swa_cols, :]
        else:
            qkv = _proj(xt, w_qkv_c[j].astype(BF16))
            o = _sb_attention(qkv.reshape(bsz, seq, 3 * d), d)
            w_o = w_o_c[j]
        w_rt, b_rt = _router_weights(w_grp[i], b_grp[i], w_exp[i], b_exp[i])
        x1, x1_slabs, ids, gates = _oproj_ln(o.reshape(n_tok, d), w_o.astype(BF16), xt, ln1_g[i], ln1_b[i],
                                             w_rt, b_rt, alpha)
        m2 = _moe_outputs(x1_slabs, ids, wg_all, wu_all, wd_all, i)
        w0 = jnp.broadcast_to(gates[0][:, None], (n_tok, LANES))
        w1 = jnp.broadcast_to(gates[1][:, None], (n_tok, LANES))
        xt = _ln_ple(x1, m2, w0, w1, ln2_g[i], ln2_b[i], p_all, i,
                     w_ple_gate[i].astype(BF16), w_ple_proj[i].astype(BF16), alpha)
    return xt.reshape(bsz, seq, d)
```

```python
import functools

import jax
import jax.numpy as jnp
import numpy as np
from jax import lax
from jax.experimental import pallas as pl
from jax.experimental.pallas import tpu as pltpu
from jax.experimental.pallas import tpu_sc as plsc

F32 = jnp.float32
BF16 = jnp.bfloat16

LANES = 128
HEAD_DIM = 64
N_HEADS = 16
N_PAIRS = N_HEADS // 2
MOBA_BLOCK = 256
MOBA_TOPK = 3
SWA_WINDOW = 128
SWA_KV_HEADS = 4
SWA_GROUP = N_HEADS // SWA_KV_HEADS
SWA_QWINS = 16
SB_TILE = 512
SB_DEAD_MASS = -100.0
N_GROUPS = 4
EXPERTS_PER_GROUP = 8
N_EXPERTS = N_GROUPS * EXPERTS_PER_GROUP
MOE_BLOCK = 512
LN_EPS = 1e-5
ROUTER_ROWS = 8 + N_EXPERTS
N_SLABS = 4
PACKED_SLABS = 2
SC_WINDOW = 128
VMEM_LIMIT = 48 * 1024 * 1024

NT_DIMS = (((1,), (1,)), ((), ()))


def _alibi_slopes():
    return jnp.asarray(np.array([2.0 ** (-8.0 * (h + 1) / N_HEADS) for h in range(N_HEADS)], dtype=np.float32))


def _params(*semantics):
    return pltpu.CompilerParams(dimension_semantics=semantics, vmem_limit_bytes=VMEM_LIMIT)


def _head_masks():
    lane = lax.broadcasted_iota(jnp.int32, (1, LANES), 1)
    return (lane < HEAD_DIM, lane >= HEAD_DIM)


def _proj_kernel(x_ref, w_ref, o_ref):
    o_ref[...] = jnp.dot(x_ref[...].astype(BF16), w_ref[...], preferred_element_type=F32).astype(o_ref.dtype)


def _proj(x, w, tm=1024, tn=3072):
    t, k = x.shape
    n = w.shape[1]
    tm = min(tm, t)
    tn = tn if n % tn == 0 else tn // 2
    return pl.pallas_call(
        _proj_kernel,
        grid=(t // tm, n // tn),
        in_specs=[pl.BlockSpec((tm, k), lambda i, j: (i, 0)), pl.BlockSpec((k, tn), lambda i, j: (0, j))],
        out_specs=pl.BlockSpec((tm, tn), lambda i, j: (i, j)),
        out_shape=jax.ShapeDtypeStruct((t, n), BF16),
        compiler_params=_params("parallel", "parallel"),
        name="qkv_proj",
    )(x, w)


def _layer_norm(y, g, b):
    mu = jnp.mean(y, axis=-1, keepdims=True)
    d = y - mu
    var = jnp.mean(d * d, axis=-1, keepdims=True)
    return d * lax.rsqrt(var + LN_EPS) * g + b


def _to_slabs(slab_ref, y):
    w = y.shape[1] // N_SLABS
    for c in range(N_SLABS):
        slab_ref[c] = y[:, c * w:(c + 1) * w]


def _from_slabs(slab_ref):
    return jnp.concatenate([slab_ref[c] for c in range(N_SLABS)], axis=1)


def _to_packed_slabs(slab_ref, y):
    half = y.shape[1] // 2
    bits = lax.bitcast_convert_type(y.astype(BF16).astype(F32), jnp.uint32)
    words = bits[:, :half] | (bits[:, half:] >> 16)
    w = half // PACKED_SLABS
    for c in range(PACKED_SLABS):
        slab_ref[c] = words[:, c * w:(c + 1) * w]


def _from_packed_slabs(slab_ref, n_valid):
    words = jnp.concatenate([slab_ref[c] for c in range(PACKED_SLABS)], axis=1)
    words = jnp.where(lax.broadcasted_iota(jnp.int32, words.shape, 0) < n_valid, words, jnp.uint32(0))
    first = lax.bitcast_convert_type(words & jnp.uint32(0xFFFF0000), F32)
    second = lax.bitcast_convert_type(words << 16, F32)
    return jnp.concatenate([first, second], axis=1).astype(BF16)


def _oproj_ln_kernel(o_ref, w_ref, x_ref, g_ref, b_ref, wr_ref, br_ref, out_ref, slab_ref, ids_ref, gates_ref, *, alpha):
    h = jnp.dot(o_ref[...], w_ref[...], preferred_element_type=F32)
    y = _layer_norm(alpha * x_ref[...] + h, g_ref[...], b_ref[...])
    out_ref[...] = y
    _to_packed_slabs(slab_ref, y)
    _route(y, wr_ref[...], br_ref[...], ids_ref, gates_ref)


def _oproj_ln(o, w, x, g, b, w_rt, b_rt, alpha, tm=1024):
    t, d = x.shape
    tm = min(tm, t)
    row = pl.BlockSpec((tm, d), lambda i: (i, 0))
    vec = pl.BlockSpec((1, d), lambda i: (0, 0))
    per_tok = pl.BlockSpec((2, tm), lambda i: (0, i))
    return pl.pallas_call(
        functools.partial(_oproj_ln_kernel, alpha=alpha),
        grid=(t // tm,),
        in_specs=[row, pl.BlockSpec((d, d), lambda i: (0, 0)), row, vec, vec,
                  pl.BlockSpec((ROUTER_ROWS, d), lambda i: (0, 0)), pl.BlockSpec((ROUTER_ROWS, 1), lambda i: (0, 0))],
        out_specs=[row, pl.BlockSpec((PACKED_SLABS, tm, d // 2 // PACKED_SLABS), lambda i: (0, i, 0)),
                   per_tok, per_tok],
        out_shape=[jax.ShapeDtypeStruct((t, d), F32),
                   jax.ShapeDtypeStruct((PACKED_SLABS, t, d // 2 // PACKED_SLABS), jnp.uint32),
                   jax.ShapeDtypeStruct((2, t), jnp.int32), jax.ShapeDtypeStruct((2, t), F32)],
        compiler_params=_params("parallel"),
        name="oproj_ln",
    )(o, w, x, g.reshape(1, d), b.reshape(1, d), w_rt, b_rt)


def _ln_ple_kernel(x_ref, m0_ref, m1_ref, w0_ref, w1_ref, g_ref, b_ref, p_ref, wg_ref, wp_ref, out_ref, *, alpha):
    m = _from_slabs(m0_ref) * w0_ref[:, 0:1] + _from_slabs(m1_ref) * w1_ref[:, 0:1]
    x2 = _layer_norm(alpha * x_ref[...] + m, g_ref[...], b_ref[...])
    gate = jax.nn.sigmoid(jnp.dot(x2.astype(BF16), wg_ref[...], preferred_element_type=F32))
    proj = jnp.dot(p_ref[...].astype(BF16), wp_ref[...], preferred_element_type=F32)
    out_ref[...] = x2 + gate * proj


def _ln_ple(x, m2, w0, w1, g, b, p, layer, wg, wp, alpha, tm=512):
    t, d = x.shape
    pd = p.shape[1]
    tm = min(tm, t)
    nt = t // tm
    p0 = layer * nt
    row = pl.BlockSpec((tm, d), lambda i: (i, 0))
    rep = pl.BlockSpec((tm, LANES), lambda i: (i, 0))
    vec = pl.BlockSpec((1, d), lambda i: (0, 0))
    return pl.pallas_call(
        functools.partial(_ln_ple_kernel, alpha=alpha),
        grid=(nt,),
        in_specs=[row, pl.BlockSpec((N_SLABS, tm, d // N_SLABS), lambda i: (0, i, 0)),
                  pl.BlockSpec((N_SLABS, tm, d // N_SLABS), lambda i: (0, i + nt, 0)), rep, rep, vec, vec,
                  pl.BlockSpec((tm, pd), lambda i: (p0 + i, 0)),
                  pl.BlockSpec((d, d), lambda i: (0, 0)), pl.BlockSpec((pd, d), lambda i: (0, 0))],
        out_specs=row,
        out_shape=jax.ShapeDtypeStruct((t, d), F32),
        compiler_params=_params("parallel"),
        name="ln_ple",
    )(x, m2, m2, w0, w1, g.reshape(1, d), b.reshape(1, d), p, wg, wp)


def _moba_kernel(slopes_ref, q_ref, k_ref, v_ref, o_ref, kmean_ref, vt_ref, bias_ref, *, n_blk, scale):
    pair = pl.program_id(1)
    qi = pl.program_id(2)
    blk = MOBA_BLOCK

    @pl.when(qi == 0)
    def _():
        for j in range(n_blk):
            kb = k_ref[j * blk:(j + 1) * blk, :].astype(F32)
            kmean_ref[j:j + 1, :] = jnp.sum(kb, axis=0, keepdims=True) * (1.0 / blk)
            vt_ref[j] = v_ref[j * blk:(j + 1) * blk, :].astype(F32).T.astype(BF16)

    kmean = kmean_ref[...].astype(BF16)
    krow = lax.broadcasted_iota(jnp.int32, (blk, blk), 0)
    qcol = lax.broadcasted_iota(jnp.int32, (blk, blk), 1)
    dpos = (qcol - krow).astype(F32)
    jrow = lax.broadcasted_iota(jnp.int32, (n_blk, blk), 0)
    past = jrow < qi
    q2 = q_ref[...]

    heads = []
    for hh, hmask in enumerate(_head_masks()):
        slope = slopes_ref[2 * pair + hh]
        qh = jnp.where(hmask, q2, jnp.zeros_like(q2))

        gate = lax.dot_general(kmean, qh, NT_DIMS, preferred_element_type=F32)
        sel = jnp.zeros((n_blk, blk), F32)
        for j in range(n_blk):
            gj = gate[j:j + 1, :]
            beats = past & ((gate > gj) | ((gate == gj) & (jrow < j)))
            n_beat = jnp.sum(beats.astype(F32), axis=0, keepdims=True)
            sel = jnp.where(jrow == j, (n_beat < MOBA_TOPK).astype(F32), sel)
        bias_ref[hh] = jnp.where(past & (sel > 0.5), 0.0, -jnp.inf)
        qs = qh * jnp.asarray(scale, BF16)
        heads.append((qs, slope * dpos, slope))

    def scores(j):
        kb = k_ref[pl.ds(pl.multiple_of(j * blk, blk), blk), :]
        return [lax.dot_general(kb, qs, NT_DIMS, preferred_element_type=F32) - sd for qs, sd, _ in heads]

    def absorb(s, j, hh, m, l, acc):
        vt = vt_ref[j][hh * HEAD_DIM:(hh + 1) * HEAD_DIM]
        m_new = jnp.maximum(m, jnp.max(s, axis=0, keepdims=True))
        shift = jnp.where(m_new == -jnp.inf, 0.0, m_new)
        a = jnp.exp(m - shift)
        p = jnp.exp(s - shift)
        l = a * l + jnp.sum(p, axis=0, keepdims=True)
        acc = a * acc + jnp.dot(vt, p.astype(BF16), preferred_element_type=F32)
        return m_new, l, acc

    def past_block(j, carry):
        new = list(scores(j + 1))
        for hh, (_, _, slope) in enumerate(heads):
            m, l, acc = carry[2 + 3 * hh:5 + 3 * hh]
            rb = bias_ref[hh, pl.ds(j, 1), :] - slope * ((qi - j) * blk).astype(F32)
            new += list(absorb(carry[hh] + rb, j, hh, m, l, acc))
        return tuple(new)

    init = list(scores(0))
    for _ in heads:
        init += [jnp.full((1, blk), -jnp.inf, F32), jnp.zeros((1, blk), F32), jnp.zeros((HEAD_DIM, blk), F32)]
    carry = lax.fori_loop(0, qi // 2, lambda t, c: past_block(2 * t + 1, past_block(2 * t, c)), tuple(init))
    carry = lax.cond(qi % 2 == 1, lambda c: past_block(qi - 1, c), lambda c: c, carry)
    outs = []
    for hh in range(len(heads)):
        m, l, acc = carry[2 + 3 * hh:5 + 3 * hh]
        m, l, acc = absorb(jnp.where(dpos >= 0, carry[hh], -jnp.inf), qi, hh, m, l, acc)
        outs.append(acc / l)
    o_ref[...] = jnp.concatenate(outs, axis=0).T.astype(o_ref.dtype)


def _moba_attention(qkv, d):
    bsz, seq, _ = qkv.shape
    assert seq % MOBA_BLOCK == 0
    n_blk = seq // MOBA_BLOCK
    ncol = d // LANES
    return pl.pallas_call(
        functools.partial(_moba_kernel, n_blk=n_blk, scale=HEAD_DIM ** -0.5),
        grid=(bsz, N_PAIRS, n_blk),
        in_specs=[pl.BlockSpec(memory_space=pltpu.SMEM),
                  pl.BlockSpec((None, MOBA_BLOCK, LANES), lambda b, p, i: (b, i, p)),
                  pl.BlockSpec((None, seq, LANES), lambda b, p, i: (b, 0, ncol + p)),
                  pl.BlockSpec((None, seq, LANES), lambda b, p, i: (b, 0, 2 * ncol + p))],
        out_specs=pl.BlockSpec((None, MOBA_BLOCK, LANES), lambda b, p, i: (b, i, p)),
        out_shape=jax.ShapeDtypeStruct((bsz, seq, d), BF16),
        scratch_shapes=[pltpu.VMEM((n_blk, LANES), F32), pltpu.VMEM((n_blk, LANES, MOBA_BLOCK), BF16),
                        pltpu.VMEM((2, n_blk, MOBA_BLOCK), F32)],
        compiler_params=_params("parallel", "parallel", "arbitrary"),
        name="moba_attn",
    )(_alibi_slopes(), qkv, qkv, qkv)


def _swa_head_order():
    order = []
    for pair in range(N_PAIRS):
        r, i = divmod(pair, SWA_GROUP)
        order += [(2 * r) * SWA_GROUP + i, (2 * r + 1) * SWA_GROUP + i]
    return order


def _swa_kernel(slopes_ref, sinks_ref, q_ref, k_ref, v_ref, o_ref, vt_ref, *, scale, n_win):
    pair = pl.program_id(1)
    n = pl.program_id(2)
    win = SWA_WINDOW
    r = pair // SWA_GROUP
    i = pair % SWA_GROUP

    @pl.when(n == 0)
    def _():
        for j in range(n_win):
            vt_ref[j] = v_ref[j * win:(j + 1) * win, :].astype(F32).T.astype(BF16)

    w0 = n * SWA_QWINS
    q0 = pl.multiple_of(w0 * win, win)
    before0 = pl.multiple_of(jnp.maximum(w0 - 1, 0) * win, win)
    k_blocks = [k_ref[pl.ds(before0, win), :]] + [k_ref[pl.ds(q0 + w * win, win), :] for w in range(SWA_QWINS)]
    vt_blocks = [vt_ref[jnp.maximum(w0 - 1, 0)]] + [vt_ref[w0 + w] for w in range(SWA_QWINS)]
    krow = lax.broadcasted_iota(jnp.int32, (2 * win, win), 0)
    qcol = lax.broadcasted_iota(jnp.int32, (2 * win, win), 1)
    dist = (qcol + win - krow).astype(F32)
    in_window = (krow > qcol) & (krow <= qcol + win)

    per_head = []
    for hh in range(2):
        head = (2 * r + hh) * SWA_GROUP + i
        per_head.append((slopes_ref[head] * dist, sinks_ref[head]))

    for w in range(SWA_QWINS):
        q2 = q_ref[w * win:(w + 1) * win, :]
        k2 = jnp.concatenate([k_blocks[w], k_blocks[w + 1]], axis=0)
        vt2 = jnp.concatenate([vt_blocks[w], vt_blocks[w + 1]], axis=1)
        allowed = in_window if w > 0 else in_window & ((krow >= win) | (n > 0))
        outs = []
        for hh, (hmask, (bias, sink)) in enumerate(zip(_head_masks(), per_head)):
            qs = jnp.where(hmask, q2, jnp.zeros_like(q2)) * jnp.asarray(scale, BF16)
            s = lax.dot_general(k2, qs, NT_DIMS, preferred_element_type=F32) - bias
            s = jnp.where(allowed, s, -jnp.inf)
            m = jnp.maximum(jnp.max(s, axis=0, keepdims=True), sink)
            p = jnp.exp(s - m)
            l = jnp.sum(p, axis=0, keepdims=True) + jnp.exp(sink - m)
            o_t = jnp.dot(vt2[hh * HEAD_DIM:(hh + 1) * HEAD_DIM], p.astype(BF16), preferred_element_type=F32)
            outs.append(o_t / l)
        o_ref[w * win:(w + 1) * win, :] = jnp.concatenate(outs, axis=0).T.astype(o_ref.dtype)


def _swa_attention(qkv, sinks, d):
    bsz, seq, _ = qkv.shape
    qb = SWA_QWINS * SWA_WINDOW
    assert seq % qb == 0
    n_win = seq // SWA_WINDOW
    ncol = d // LANES
    kv_col = SWA_KV_HEADS * HEAD_DIM // LANES
    return pl.pallas_call(
        functools.partial(_swa_kernel, scale=HEAD_DIM ** -0.5, n_win=n_win),
        grid=(bsz, N_PAIRS, seq // qb),
        in_specs=[pl.BlockSpec(memory_space=pltpu.SMEM), pl.BlockSpec(memory_space=pltpu.SMEM),
                  pl.BlockSpec((None, qb, LANES), lambda b, p, i: (b, i, p)),
                  pl.BlockSpec((None, seq, LANES), lambda b, p, i: (b, 0, ncol + p // SWA_GROUP)),
                  pl.BlockSpec((None, seq, LANES), lambda b, p, i: (b, 0, ncol + kv_col + p // SWA_GROUP))],
        out_specs=pl.BlockSpec((None, qb, LANES), lambda b, p, i: (b, i, p)),
        out_shape=jax.ShapeDtypeStruct((bsz, seq, d), BF16),
        scratch_shapes=[pltpu.VMEM((n_win, LANES, SWA_WINDOW), BF16)],
        compiler_params=_params("parallel", "parallel", "arbitrary"),
        name="swa_attn",
    )(_alibi_slopes(), sinks, qkv, qkv, qkv)


def _log_sigmoid(z):
    return jnp.minimum(z, 0.0) - jnp.log(1.0 + jnp.exp(-jnp.abs(z)))


def _suffix_sums(lk, upper):
    hi = lk.astype(BF16)
    lo = (lk - hi.astype(F32)).astype(BF16)
    n = lk.shape[0]
    both = jnp.dot(jnp.concatenate([hi, lo], axis=0), upper, preferred_element_type=F32)
    return both[:n] + both[n:]


def _sb_tile(qh, kb, vb, upper, later, strict):
    z = lax.dot_general(qh, kb, NT_DIMS, preferred_element_type=F32)
    ls = _log_sigmoid(z)
    lk = ls - z
    if strict is not None:
        lk = jnp.where(strict, lk, 0.0)
    after = _suffix_sums(lk, upper)
    if later is not None:
        after = after + later
    a = jnp.exp(ls + after)
    if strict is not None:
        a = jnp.where(strict, a, 0.0)
    mass = jnp.sum(lk, axis=1, keepdims=True)
    return jnp.dot(a.astype(BF16), vb, preferred_element_type=F32), (mass if later is None else later + mass)


def _sb_kernel(q_ref, k_ref, v_ref, o_ref, *, scale):
    c = pl.program_id(2)
    blk = SB_TILE
    row = lax.broadcasted_iota(jnp.int32, (blk, blk), 0)
    col = lax.broadcasted_iota(jnp.int32, (blk, blk), 1)
    strict = col < row
    upper = (row > col).astype(BF16)
    own0 = pl.multiple_of(c * blk, blk)
    k_own = k_ref[pl.ds(own0, blk), :]
    v_own = v_ref[pl.ds(own0, blk), :]
    q2 = q_ref[...]
    masks = _head_masks()
    qhs = [jnp.where(hmask, q2, jnp.zeros_like(q2)) * jnp.asarray(scale, BF16) for hmask in masks]

    carry = []
    for qh in qhs:
        acc, later = _sb_tile(qh, k_own, v_own, upper, None, strict)
        carry += [later, acc]

    def earlier_block(state):
        step, carry = state[0], state[1:]
        j0 = pl.multiple_of((c - 1 - step) * blk, blk)
        kb = k_ref[pl.ds(j0, blk), :]
        vb = v_ref[pl.ds(j0, blk), :]
        new = [step + 1]
        for hh, qh in enumerate(qhs):
            contrib, later = _sb_tile(qh, kb, vb, upper, carry[2 * hh], None)
            new += [later, carry[2 * hh + 1] + contrib]
        return tuple(new)

    def weights_can_be_nonzero(state):
        alive = jnp.maximum(jnp.max(state[1]), jnp.max(state[3])) >= SB_DEAD_MASS
        return (state[0] < c) & alive

    carry = lax.while_loop(weights_can_be_nonzero, earlier_block, (jnp.int32(0),) + tuple(carry))[1:]
    o_ref[...] = jnp.where(masks[0], carry[1], carry[3]).astype(o_ref.dtype)


def _sb_attention(qkv, d):
    bsz, seq, _ = qkv.shape
    blk = min(SB_TILE, seq)
    assert seq % blk == 0 and blk == SB_TILE
    ncol = d // LANES
    return pl.pallas_call(
        functools.partial(_sb_kernel, scale=HEAD_DIM ** -0.5),
        grid=(bsz, N_PAIRS, seq // blk),
        in_specs=[pl.BlockSpec((None, blk, LANES), lambda b, p, i: (b, i, p)),
                  pl.BlockSpec((None, seq, LANES), lambda b, p, i: (b, 0, ncol + p)),
                  pl.BlockSpec((None, seq, LANES), lambda b, p, i: (b, 0, 2 * ncol + p))],
        out_specs=pl.BlockSpec((None, blk, LANES), lambda b, p, i: (b, i, p)),
        out_shape=jax.ShapeDtypeStruct((bsz, seq, d), BF16),
        compiler_params=_params("parallel", "parallel", "arbitrary"),
        name="sb_attn",
    )(qkv, qkv, qkv)


def _split_bf16(a):
    hi = a.astype(BF16)
    return hi, (a - hi.astype(F32)).astype(BF16)


def _route(x, w, b, ids_ref, gates_ref):
    xh, xl = _split_bf16(x)
    wh, wl = _split_bf16(w)
    logits = (lax.dot_general(wh, xh, NT_DIMS, preferred_element_type=F32)
              + lax.dot_general(wh, xl, NT_DIMS, preferred_element_type=F32)
              + lax.dot_general(wl, xh, NT_DIMS, preferred_element_type=F32)) + b
    tm = logits.shape[1]
    gl = logits[0:N_GROUPS, :]
    g_row = lax.broadcasted_iota(jnp.int32, (N_GROUPS, tm), 0).astype(F32)
    g_max = jnp.max(gl, axis=0, keepdims=True)
    g_p = 1.0 / jnp.sum(jnp.exp(gl - g_max), axis=0, keepdims=True)
    g_idx = jnp.min(jnp.where(gl == g_max, g_row, float(N_GROUPS)), axis=0, keepdims=True)

    in_grp = jnp.zeros((EXPERTS_PER_GROUP, tm), F32)
    for g in range(N_GROUPS):
        lo = 8 + g * EXPERTS_PER_GROUP
        in_grp = jnp.where(g_idx == float(g), logits[lo:lo + EXPERTS_PER_GROUP, :], in_grp)
    e_exp = jnp.exp(in_grp - jnp.max(in_grp, axis=0, keepdims=True))
    e_p = e_exp / jnp.sum(e_exp, axis=0, keepdims=True)
    e_row = lax.broadcasted_iota(jnp.int32, (EXPERTS_PER_GROUP, tm), 0).astype(F32)
    p1 = jnp.max(e_p, axis=0, keepdims=True)
    i1 = jnp.min(jnp.where(e_p == p1, e_row, float(EXPERTS_PER_GROUP)), axis=0, keepdims=True)
    rest = jnp.where(e_row == i1, -1.0, e_p)
    p2 = jnp.max(rest, axis=0, keepdims=True)
    i2 = jnp.min(jnp.where(rest == p2, e_row, float(EXPERTS_PER_GROUP)), axis=0, keepdims=True)
    den = p1 + p2
    base = g_idx * float(EXPERTS_PER_GROUP)
    ids_ref[0:1, :] = (base + i1).astype(jnp.int32)
    ids_ref[1:2, :] = (base + i2).astype(jnp.int32)
    gates_ref[0:1, :] = g_p * (p1 / den)
    gates_ref[1:2, :] = g_p * (p2 / den)


def _sc_gather_rows(table, idx):
    n = idx.shape[0]
    width = table.shape[1]
    mesh = plsc.VectorSubcoreMesh(core_axis_name="core", subcore_axis_name="subcore")

    @pl.kernel(out_type=jax.ShapeDtypeStruct((n, width), table.dtype), mesh=mesh, scratch_types=[])
    def gather(table_hbm, idx_hbm, out_hbm):
        def step(idx_vmem, out_vmem):
            pltpu.sync_copy(table_hbm.at[idx_vmem.at[0]], out_vmem)

        pltpu.emit_pipeline(
            step,
            grid=(n // SC_WINDOW,),
            in_specs=[pl.BlockSpec((1, SC_WINDOW), index_map=lambda i: (0, i))],
            out_specs=[pl.BlockSpec((SC_WINDOW, width), index_map=lambda i: (i, 0))],
            core_axis_name=("core", "subcore"),
            dimension_semantics=(pltpu.PARALLEL,),
        )(idx_hbm, out_hbm)

    return gather(table, idx.reshape(1, n))


def _gather_slabs(slabs, rows):
    n_slab, n_rows, width = slabs.shape
    idx = (jnp.arange(n_slab, dtype=jnp.int32)[:, None] * n_rows + rows[None, :]).reshape(-1)
    return _sc_gather_rows(slabs.reshape(n_slab * n_rows, width), idx).reshape(n_slab, rows.shape[0], width)


def _scatter_slabs(slabs, rows, n_out):
    n_slab, n_tok, width = slabs.shape
    n = n_slab * rows.shape[0]
    tok_blocks = n_tok // SC_WINDOW
    idx = (jnp.arange(n_slab, dtype=jnp.int32)[:, None] * n_out + rows[None, :]).reshape(1, n)
    mesh = plsc.VectorSubcoreMesh(core_axis_name="core", subcore_axis_name="subcore")

    @pl.kernel(out_type=jax.ShapeDtypeStruct((n_slab * n_out, width), slabs.dtype), mesh=mesh, scratch_types=[])
    def scatter(src_hbm, idx_hbm, out_hbm):
        def step(src_vmem, idx_vmem):
            pltpu.sync_copy(src_vmem, out_hbm.at[idx_vmem.at[0]])

        pltpu.emit_pipeline(
            step,
            grid=(n // SC_WINDOW,),
            in_specs=[pl.BlockSpec((SC_WINDOW, width),
                                   index_map=lambda i: ((i // (2 * tok_blocks)) * tok_blocks + i % tok_blocks, 0)),
                      pl.BlockSpec((1, SC_WINDOW), index_map=lambda i: (0, i))],
            out_specs=[],
            core_axis_name=("core", "subcore"),
            dimension_semantics=(pltpu.PARALLEL,),
        )(src_hbm, idx_hbm)

    return scatter(slabs.reshape(n_slab * n_tok, width), idx).reshape(n_slab, n_out, width)


def _expert_kernel(block_e_ref, valid_ref, nused_ref, x_ref, wg_ref, wu_ref, wd_ref, y_ref, wg_bf, wu_bf, wd_bf):
    i = pl.program_id(0)

    @pl.when(i < nused_ref[0])
    def _():
        @pl.when((i == 0) | (block_e_ref[i] != block_e_ref[jnp.maximum(i - 1, 0)]))
        def _():
            wg_bf[...] = wg_ref[...].astype(BF16)
            wu_bf[...] = wu_ref[...].astype(BF16)
            wd_bf[...] = wd_ref[...].astype(BF16)

        xb = _from_packed_slabs(x_ref, valid_ref[i])
        g = jnp.dot(xb, wg_bf[...], preferred_element_type=F32)
        u = jnp.dot(xb, wu_bf[...], preferred_element_type=F32)
        h = (g * jax.nn.sigmoid(g)) * u
        _to_slabs(y_ref, jnp.dot(h.astype(BF16), wd_bf[...], preferred_element_type=F32))


def _expert_ffn(block_e, valid, nused, xs, wg, wu, wd, layer):
    n_packed, n_rows, width = xs.shape
    n_slab = N_SLABS
    d, de = wg.shape[1], wg.shape[2]
    n_blocks = n_rows // MOE_BLOCK
    base = layer * N_EXPERTS

    def rows(i, be, nv, nu):
        return (0, jnp.minimum(i, nu[0] - 1), 0)

    def expert(i, be, nv, nu):
        return (base + be[i], 0, 0)

    return pl.pallas_call(
        _expert_kernel,
        grid_spec=pltpu.PrefetchScalarGridSpec(
            num_scalar_prefetch=3,
            grid=(n_blocks,),
            in_specs=[pl.BlockSpec((n_packed, MOE_BLOCK, width), rows),
                      pl.BlockSpec((None, d, de), expert),
                      pl.BlockSpec((None, d, de), expert),
                      pl.BlockSpec((None, de, d), expert)],
            out_specs=pl.BlockSpec((n_slab, MOE_BLOCK, width), rows),
            scratch_shapes=[pltpu.VMEM((d, de), BF16), pltpu.VMEM((d, de), BF16), pltpu.VMEM((de, d), BF16)]),
        out_shape=jax.ShapeDtypeStruct((n_slab, n_rows, width), F32),
        compiler_params=_params("arbitrary"),
        name="moe_experts",
    )(block_e, valid, nused, xs, wg, wu, wd)


def _plan_kernel(e_ref, row_ref, block_e_ref, valid_ref, nused_ref):
    e_all = e_ref[...]
    n_chunk = e_all.shape[0]
    lane_before = (lax.broadcasted_iota(jnp.int32, (LANES, LANES), 0)
                   < lax.broadcasted_iota(jnp.int32, (LANES, LANES), 1))
    within = jnp.concatenate([lane_before.astype(BF16), jnp.ones((LANES, LANES), BF16)], axis=1)
    chunk_before = (lax.broadcasted_iota(jnp.int32, (n_chunk, n_chunk), 1)
                    < lax.broadcasted_iota(jnp.int32, (n_chunk, n_chunk), 0)).astype(BF16)
    block_lane = lax.broadcasted_iota(jnp.int32, block_e_ref.shape, 1).astype(F32)

    row = jnp.zeros(e_all.shape, F32)
    blocks_done = jnp.zeros((1, LANES), F32)
    experts_done = jnp.zeros(block_e_ref.shape, F32)
    valid = jnp.zeros(block_e_ref.shape, F32)
    for e in range(N_EXPERTS):
        mine = e_all == e
        counts = jnp.dot(mine.astype(BF16), within, preferred_element_type=F32)
        in_chunk, chunk_total = counts[:, :LANES], counts[:, LANES:]
        earlier_chunks = jnp.dot(chunk_before, chunk_total.astype(BF16), preferred_element_type=F32)
        n_mine = earlier_chunks[n_chunk - 1:, :] + chunk_total[n_chunk - 1:, :]
        row = jnp.where(mine, blocks_done * MOE_BLOCK + earlier_chunks + in_chunk, row)
        left = n_mine[:, 0:1] - (block_lane - blocks_done[:, 0:1]) * MOE_BLOCK
        valid = jnp.where(block_lane >= blocks_done[:, 0:1], jnp.clip(left, 0.0, float(MOE_BLOCK)), valid)
        blocks_done = blocks_done + jnp.floor((n_mine + (MOE_BLOCK - 1)) * (1.0 / MOE_BLOCK))
        experts_done = experts_done + (block_lane >= blocks_done[:, 0:1]).astype(F32)
    row_ref[...] = row.astype(jnp.int32)
    block_e_ref[...] = jnp.minimum(experts_done, N_EXPERTS - 1).astype(jnp.int32)
    valid_ref[...] = valid.astype(jnp.int32)
    nused_ref[...] = blocks_done.astype(jnp.int32)


def _dispatch_plan(ids, n_tok):
    n_asg = 2 * n_tok
    n_blocks = n_asg // MOE_BLOCK + N_EXPERTS
    block_lanes = -(-n_blocks // LANES) * LANES
    per_block = jax.ShapeDtypeStruct((1, block_lanes), jnp.int32)
    rows, block_e, valid, nused = pl.pallas_call(
        _plan_kernel,
        out_shape=[jax.ShapeDtypeStruct((n_asg // LANES, LANES), jnp.int32), per_block, per_block,
                   jax.ShapeDtypeStruct((1, LANES), jnp.int32)],
        compiler_params=pltpu.CompilerParams(vmem_limit_bytes=VMEM_LIMIT),
        name="moe_plan",
    )(ids.reshape(n_asg // LANES, LANES))
    return block_e[0, :n_blocks], valid[0, :n_blocks], nused[0, :1], rows.reshape(n_asg)


def _moe_outputs(x_slabs, ids, wg, wu, wd, layer):
    block_e, valid, nused, asg_row = _dispatch_plan(ids, ids.shape[1])
    xs = _scatter_slabs(x_slabs, asg_row, block_e.shape[0] * MOE_BLOCK)
    ys = _expert_ffn(block_e, valid, nused, xs, wg, wu, wd, layer)
    return _gather_slabs(ys, asg_row)


def _router_weights(w_grp, b_grp, w_exp, b_exp):
    d = w_grp.shape[0]
    pad = 8 - N_GROUPS
    w_rt = jnp.concatenate([w_grp.T, jnp.zeros((pad, d), F32), w_exp.T], axis=0)
    b_rt = jnp.concatenate([b_grp, jnp.zeros((pad,), F32), b_exp]).reshape(ROUTER_ROWS, 1)
    return w_rt, b_rt


def kernel(x, p, w_qkv_a, w_o_a, w_qkv_b, w_o_b, sinks_b, w_qkv_c, w_o_c, ln1_g, ln1_b, ln2_g, ln2_b, w_grp, b_grp, w_exp, b_exp, w_e_gate, w_e_up, w_e_down, w_ple_gate, w_ple_proj):
    bsz, seq, d = x.shape
    depth = p.shape[0]
    n_tok = bsz * seq
    alpha = (2.0 * depth) ** 0.25
    kv_w = SWA_KV_HEADS * HEAD_DIM
    swa_cols = jnp.asarray(np.concatenate([np.arange(h * HEAD_DIM, (h + 1) * HEAD_DIM) for h in _swa_head_order()]))

    n_all = depth * N_EXPERTS
    wg_all = w_e_gate.reshape(n_all, d, -1)
    wu_all = w_e_up.reshape(n_all, d, -1)
    wd_all = w_e_down.reshape(n_all, -1, d)
    p_all = p.reshape(depth * n_tok, -1)

    xt = x.reshape(n_tok, d)
    for i in range(depth):
        mixer, j = i % 3, i // 3
        if mixer == 0:
            qkv = _proj(xt, w_qkv_a[j].astype(BF16))
            o = _moba_attention(qkv.reshape(bsz, seq, 3 * d), d)
            w_o = w_o_a[j]
        elif mixer == 1:
            w_qkv = jnp.concatenate([w_qkv_b[j][:, :d][:, swa_cols], w_qkv_b[j][:, d:]], axis=1)
            qkv = _proj(xt, w_qkv.astype(BF16))
            o = _swa_attention(qkv.reshape(bsz, seq, d + 2 * kv_w), sinks_b[j], d)
            w_o = w_o_b[j][swa_cols, :]
        else:
            qkv = _proj(xt, w_qkv_c[j].astype(BF16))
            o = _sb_attention(qkv.reshape(bsz, seq, 3 * d), d)
            w_o = w_o_c[j]
        w_rt, b_rt = _router_weights(w_grp[i], b_grp[i], w_exp[i], b_exp[i])
        x1, x1_slabs, ids, gates = _oproj_ln(o.reshape(n_tok, d), w_o.astype(BF16), xt, ln1_g[i], ln1_b[i],
                                             w_rt, b_rt, alpha)
        m2 = _moe_outputs(x1_slabs, ids, wg_all, wu_all, wd_all, i)
        w0 = jnp.broadcast_to(gates[0][:, None], (n_tok, LANES))
        w1 = jnp.broadcast_to(gates[1][:, None], (n_tok, LANES))
        xt = _ln_ple(x1, m2, w0, w1, ln2_g[i], ln2_b[i], p_all, i,
                     w_ple_gate[i].astype(BF16), w_ple_proj[i].astype(BF16), alpha)
    return xt.reshape(bsz, seq, d)
```

```python
import functools

import jax
import jax.numpy as jnp
import numpy as np
from jax import lax
from jax.experimental import pallas as pl
from jax.experimental.pallas import tpu as pltpu
from jax.experimental.pallas import tpu_sc as plsc

F32 = jnp.float32
BF16 = jnp.bfloat16

LANES = 128
HEAD_DIM = 64
N_HEADS = 16
N_PAIRS = N_HEADS // 2
MOBA_BLOCK = 256
MOBA_TOPK = 3
SWA_WINDOW = 128
SWA_KV_HEADS = 4
SWA_GROUP = N_HEADS // SWA_KV_HEADS
SWA_QWINS = 16
SB_TILE = 512
SB_DEAD_MASS = -100.0
N_GROUPS = 4
EXPERTS_PER_GROUP = 8
N_EXPERTS = N_GROUPS * EXPERTS_PER_GROUP
MOE_BLOCK = 1024
LN_EPS = 1e-5
ROUTER_ROWS = 8 + N_EXPERTS
N_SLABS = 4
PACKED_SLABS = 2
SC_WINDOW = 128
VMEM_LIMIT = 48 * 1024 * 1024

NT_DIMS = (((1,), (1,)), ((), ()))


def _alibi_slopes():
    return jnp.asarray(np.array([2.0 ** (-8.0 * (h + 1) / N_HEADS) for h in range(N_HEADS)], dtype=np.float32))


def _params(*semantics):
    return pltpu.CompilerParams(dimension_semantics=semantics, vmem_limit_bytes=VMEM_LIMIT)


def _head_masks():
    lane = lax.broadcasted_iota(jnp.int32, (1, LANES), 1)
    return (lane < HEAD_DIM, lane >= HEAD_DIM)


def _proj_kernel(x_ref, w_ref, o_ref):
    o_ref[...] = jnp.dot(x_ref[...].astype(BF16), w_ref[...], preferred_element_type=F32).astype(o_ref.dtype)


def _proj(x, w, tm=1024, tn=3072):
    t, k = x.shape
    n = w.shape[1]
    tm = min(tm, t)
    tn = tn if n % tn == 0 else tn // 2
    return pl.pallas_call(
        _proj_kernel,
        grid=(t // tm, n // tn),
        in_specs=[pl.BlockSpec((tm, k), lambda i, j: (i, 0)), pl.BlockSpec((k, tn), lambda i, j: (0, j))],
        out_specs=pl.BlockSpec((tm, tn), lambda i, j: (i, j)),
        out_shape=jax.ShapeDtypeStruct((t, n), BF16),
        compiler_params=_params("parallel", "parallel"),
        name="qkv_proj",
    )(x, w)


def _layer_norm(y, g, b):
    mu = jnp.mean(y, axis=-1, keepdims=True)
    d = y - mu
    var = jnp.mean(d * d, axis=-1, keepdims=True)
    return d * lax.rsqrt(var + LN_EPS) * g + b


def _to_slabs(slab_ref, y):
    w = y.shape[1] // N_SLABS
    for c in range(N_SLABS):
        slab_ref[c] = y[:, c * w:(c + 1) * w]


def _from_slabs(slab_ref):
    return jnp.concatenate([slab_ref[c] for c in range(N_SLABS)], axis=1)


def _to_packed_slabs(slab_ref, y):
    half = y.shape[1] // 2
    bits = lax.bitcast_convert_type(y.astype(BF16).astype(F32), jnp.uint32)
    words = bits[:, :half] | (bits[:, half:] >> 16)
    w = half // PACKED_SLABS
    for c in range(PACKED_SLABS):
        slab_ref[c] = words[:, c * w:(c + 1) * w]


def _from_packed_slabs(slab_ref, n_valid):
    words = jnp.concatenate([slab_ref[c] for c in range(PACKED_SLABS)], axis=1)
    words = jnp.where(lax.broadcasted_iota(jnp.int32, words.shape, 0) < n_valid, words, jnp.uint32(0))
    first = lax.bitcast_convert_type(words & jnp.uint32(0xFFFF0000), F32)
    second = lax.bitcast_convert_type(words << 16, F32)
    return jnp.concatenate([first, second], axis=1).astype(BF16)


def _oproj_ln_kernel(o_ref, w_ref, x_ref, g_ref, b_ref, wr_ref, br_ref, out_ref, slab_ref, ids_ref, gates_ref, *, alpha):
    h = jnp.dot(o_ref[...], w_ref[...], preferred_element_type=F32)
    y = _layer_norm(alpha * x_ref[...] + h, g_ref[...], b_ref[...])
    out_ref[...] = y
    _to_packed_slabs(slab_ref, y)
    _route(y, wr_ref[...], br_ref[...], ids_ref, gates_ref)


def _oproj_ln(o, w, x, g, b, w_rt, b_rt, alpha, tm=1024):
    t, d = x.shape
    tm = min(tm, t)
    row = pl.BlockSpec((tm, d), lambda i: (i, 0))
    vec = pl.BlockSpec((1, d), lambda i: (0, 0))
    per_tok = pl.BlockSpec((2, tm), lambda i: (0, i))
    return pl.pallas_call(
        functools.partial(_oproj_ln_kernel, alpha=alpha),
        grid=(t // tm,),
        in_specs=[row, pl.BlockSpec((d, d), lambda i: (0, 0)), row, vec, vec,
                  pl.BlockSpec((ROUTER_ROWS, d), lambda i: (0, 0)), pl.BlockSpec((ROUTER_ROWS, 1), lambda i: (0, 0))],
        out_specs=[row, pl.BlockSpec((PACKED_SLABS, tm, d // 2 // PACKED_SLABS), lambda i: (0, i, 0)),
                   per_tok, per_tok],
        out_shape=[jax.ShapeDtypeStruct((t, d), F32),
                   jax.ShapeDtypeStruct((PACKED_SLABS, t, d // 2 // PACKED_SLABS), jnp.uint32),
                   jax.ShapeDtypeStruct((2, t), jnp.int32), jax.ShapeDtypeStruct((2, t), F32)],
        compiler_params=_params("parallel"),
        name="oproj_ln",
    )(o, w, x, g.reshape(1, d), b.reshape(1, d), w_rt, b_rt)


def _ln_ple_kernel(x_ref, m0_ref, m1_ref, w0_ref, w1_ref, g_ref, b_ref, p_ref, wg_ref, wp_ref, out_ref, *, alpha):
    m = _from_slabs(m0_ref) * w0_ref[:, 0:1] + _from_slabs(m1_ref) * w1_ref[:, 0:1]
    x2 = _layer_norm(alpha * x_ref[...] + m, g_ref[...], b_ref[...])
    gate = jax.nn.sigmoid(jnp.dot(x2.astype(BF16), wg_ref[...], preferred_element_type=F32))
    proj = jnp.dot(p_ref[...].astype(BF16), wp_ref[...], preferred_element_type=F32)
    out_ref[...] = x2 + gate * proj


def _ln_ple(x, m2, w0, w1, g, b, p, layer, wg, wp, alpha, tm=512):
    t, d = x.shape
    pd = p.shape[1]
    tm = min(tm, t)
    nt = t // tm
    p0 = layer * nt
    row = pl.BlockSpec((tm, d), lambda i: (i, 0))
    rep = pl.BlockSpec((tm, LANES), lambda i: (i, 0))
    vec = pl.BlockSpec((1, d), lambda i: (0, 0))
    return pl.pallas_call(
        functools.partial(_ln_ple_kernel, alpha=alpha),
        grid=(nt,),
        in_specs=[row, pl.BlockSpec((N_SLABS, tm, d // N_SLABS), lambda i: (0, i, 0)),
                  pl.BlockSpec((N_SLABS, tm, d // N_SLABS), lambda i: (0, i + nt, 0)), rep, rep, vec, vec,
                  pl.BlockSpec((tm, pd), lambda i: (p0 + i, 0)),
                  pl.BlockSpec((d, d), lambda i: (0, 0)), pl.BlockSpec((pd, d), lambda i: (0, 0))],
        out_specs=row,
        out_shape=jax.ShapeDtypeStruct((t, d), F32),
        compiler_params=_params("parallel"),
        name="ln_ple",
    )(x, m2, m2, w0, w1, g.reshape(1, d), b.reshape(1, d), p, wg, wp)


def _moba_kernel(slopes_ref, q_ref, k_ref, v_ref, o_ref, kmean_ref, vt_ref, bias_ref, *, n_blk, scale):
    pair = pl.program_id(1)
    qi = pl.program_id(2)
    blk = MOBA_BLOCK

    @pl.when(qi == 0)
    def _():
        for j in range(n_blk):
            kb = k_ref[j * blk:(j + 1) * blk, :].astype(F32)
            kmean_ref[j:j + 1, :] = jnp.sum(kb, axis=0, keepdims=True) * (1.0 / blk)
            vt_ref[j] = v_ref[j * blk:(j + 1) * blk, :].astype(F32).T.astype(BF16)

    kmean = kmean_ref[...].astype(BF16)
    krow = lax.broadcasted_iota(jnp.int32, (blk, blk), 0)
    qcol = lax.broadcasted_iota(jnp.int32, (blk, blk), 1)
    dpos = (qcol - krow).astype(F32)
    jrow = lax.broadcasted_iota(jnp.int32, (n_blk, blk), 0)
    past = jrow < qi
    q2 = q_ref[...]

    heads = []
    for hh, hmask in enumerate(_head_masks()):
        slope = slopes_ref[2 * pair + hh]
        qh = jnp.where(hmask, q2, jnp.zeros_like(q2))

        gate = lax.dot_general(kmean, qh, NT_DIMS, preferred_element_type=F32)
        sel = jnp.zeros((n_blk, blk), F32)
        for j in range(n_blk):
            gj = gate[j:j + 1, :]
            beats = past & ((gate > gj) | ((gate == gj) & (jrow < j)))
            n_beat = jnp.sum(beats.astype(F32), axis=0, keepdims=True)
            sel = jnp.where(jrow == j, (n_beat < MOBA_TOPK).astype(F32), sel)
        bias_ref[hh] = jnp.where(past & (sel > 0.5), 0.0, -jnp.inf)
        qs = qh * jnp.asarray(scale, BF16)
        heads.append((qs, slope * dpos, slope))

    def scores(j):
        kb = k_ref[pl.ds(pl.multiple_of(j * blk, blk), blk), :]
        return [lax.dot_general(kb, qs, NT_DIMS, preferred_element_type=F32) - sd for qs, sd, _ in heads]

    def absorb(s, j, hh, m, l, acc):
        vt = vt_ref[j][hh * HEAD_DIM:(hh + 1) * HEAD_DIM]
        m_new = jnp.maximum(m, jnp.max(s, axis=0, keepdims=True))
        shift = jnp.where(m_new == -jnp.inf, 0.0, m_new)
        a = jnp.exp(m - shift)
        p = jnp.exp(s - shift)
        l = a * l + jnp.sum(p, axis=0, keepdims=True)
        acc = a * acc + jnp.dot(vt, p.astype(BF16), preferred_element_type=F32)
        return m_new, l, acc

    def past_block(j, carry):
        new = list(scores(j + 1))
        for hh, (_, _, slope) in enumerate(heads):
            m, l, acc = carry[2 + 3 * hh:5 + 3 * hh]
            rb = bias_ref[hh, pl.ds(j, 1), :] - slope * ((qi - j) * blk).astype(F32)
            new += list(absorb(carry[hh] + rb, j, hh, m, l, acc))
        return tuple(new)

    init = list(scores(0))
    for _ in heads:
        init += [jnp.full((1, blk), -jnp.inf, F32), jnp.zeros((1, blk), F32), jnp.zeros((HEAD_DIM, blk), F32)]
    carry = lax.fori_loop(0, qi // 2, lambda t, c: past_block(2 * t + 1, past_block(2 * t, c)), tuple(init))
    carry = lax.cond(qi % 2 == 1, lambda c: past_block(qi - 1, c), lambda c: c, carry)
    outs = []
    for hh in range(len(heads)):
        m, l, acc = carry[2 + 3 * hh:5 + 3 * hh]
        m, l, acc = absorb(jnp.where(dpos >= 0, carry[hh], -jnp.inf), qi, hh, m, l, acc)
        outs.append(acc / l)
    o_ref[...] = jnp.concatenate(outs, axis=0).T.astype(o_ref.dtype)


def _moba_attention(qkv, d):
    bsz, seq, _ = qkv.shape
    assert seq % MOBA_BLOCK == 0
    n_blk = seq // MOBA_BLOCK
    ncol = d // LANES
    return pl.pallas_call(
        functools.partial(_moba_kernel, n_blk=n_blk, scale=HEAD_DIM ** -0.5),
        grid=(bsz, N_PAIRS, n_blk),
        in_specs=[pl.BlockSpec(memory_space=pltpu.SMEM),
                  pl.BlockSpec((None, MOBA_BLOCK, LANES), lambda b, p, i: (b, i, p)),
                  pl.BlockSpec((None, seq, LANES), lambda b, p, i: (b, 0, ncol + p)),
                  pl.BlockSpec((None, seq, LANES), lambda b, p, i: (b, 0, 2 * ncol + p))],
        out_specs=pl.BlockSpec((None, MOBA_BLOCK, LANES), lambda b, p, i: (b, i, p)),
        out_shape=jax.ShapeDtypeStruct((bsz, seq, d), BF16),
        scratch_shapes=[pltpu.VMEM((n_blk, LANES), F32), pltpu.VMEM((n_blk, LANES, MOBA_BLOCK), BF16),
                        pltpu.VMEM((2, n_blk, MOBA_BLOCK), F32)],
        compiler_params=_params("parallel", "parallel", "arbitrary"),
        name="moba_attn",
    )(_alibi_slopes(), qkv, qkv, qkv)


def _swa_head_order():
    order = []
    for pair in range(N_PAIRS):
        r, i = divmod(pair, SWA_GROUP)
        order += [(2 * r) * SWA_GROUP + i, (2 * r + 1) * SWA_GROUP + i]
    return order


def _swa_kernel(slopes_ref, sinks_ref, q_ref, k_ref, v_ref, o_ref, vt_ref, *, scale, n_win):
    pair = pl.program_id(1)
    n = pl.program_id(2)
    win = SWA_WINDOW
    r = pair // SWA_GROUP
    i = pair % SWA_GROUP

    @pl.when(n == 0)
    def _():
        for j in range(n_win):
            vt_ref[j] = v_ref[j * win:(j + 1) * win, :].astype(F32).T.astype(BF16)

    w0 = n * SWA_QWINS
    q0 = pl.multiple_of(w0 * win, win)
    before0 = pl.multiple_of(jnp.maximum(w0 - 1, 0) * win, win)
    k_blocks = [k_ref[pl.ds(before0, win), :]] + [k_ref[pl.ds(q0 + w * win, win), :] for w in range(SWA_QWINS)]
    vt_blocks = [vt_ref[jnp.maximum(w0 - 1, 0)]] + [vt_ref[w0 + w] for w in range(SWA_QWINS)]
    krow = lax.broadcasted_iota(jnp.int32, (2 * win, win), 0)
    qcol = lax.broadcasted_iota(jnp.int32, (2 * win, win), 1)
    dist = (qcol + win - krow).astype(F32)
    in_window = (krow > qcol) & (krow <= qcol + win)

    per_head = []
    for hh in range(2):
        head = (2 * r + hh) * SWA_GROUP + i
        per_head.append((slopes_ref[head] * dist, sinks_ref[head]))

    for w in range(SWA_QWINS):
        q2 = q_ref[w * win:(w + 1) * win, :]
        k2 = jnp.concatenate([k_blocks[w], k_blocks[w + 1]], axis=0)
        vt2 = jnp.concatenate([vt_blocks[w], vt_blocks[w + 1]], axis=1)
        allowed = in_window if w > 0 else in_window & ((krow >= win) | (n > 0))
        outs = []
        for hh, (hmask, (bias, sink)) in enumerate(zip(_head_masks(), per_head)):
            qs = jnp.where(hmask, q2, jnp.zeros_like(q2)) * jnp.asarray(scale, BF16)
            s = lax.dot_general(k2, qs, NT_DIMS, preferred_element_type=F32) - bias
            s = jnp.where(allowed, s, -jnp.inf)
            m = jnp.maximum(jnp.max(s, axis=0, keepdims=True), sink)
            p = jnp.exp(s - m)
            l = jnp.sum(p, axis=0, keepdims=True) + jnp.exp(sink - m)
            o_t = jnp.dot(vt2[hh * HEAD_DIM:(hh + 1) * HEAD_DIM], p.astype(BF16), preferred_element_type=F32)
            outs.append(o_t / l)
        o_ref[w * win:(w + 1) * win, :] = jnp.concatenate(outs, axis=0).T.astype(o_ref.dtype)


def _swa_attention(qkv, sinks, d):
    bsz, seq, _ = qkv.shape
    qb = SWA_QWINS * SWA_WINDOW
    assert seq % qb == 0
    n_win = seq // SWA_WINDOW
    ncol = d // LANES
    kv_col = SWA_KV_HEADS * HEAD_DIM // LANES
    return pl.pallas_call(
        functools.partial(_swa_kernel, scale=HEAD_DIM ** -0.5, n_win=n_win),
        grid=(bsz, N_PAIRS, seq // qb),
        in_specs=[pl.BlockSpec(memory_space=pltpu.SMEM), pl.BlockSpec(memory_space=pltpu.SMEM),
                  pl.BlockSpec((None, qb, LANES), lambda b, p, i: (b, i, p)),
                  pl.BlockSpec((None, seq, LANES), lambda b, p, i: (b, 0, ncol + p // SWA_GROUP)),
                  pl.BlockSpec((None, seq, LANES), lambda b, p, i: (b, 0, ncol + kv_col + p // SWA_GROUP))],
        out_specs=pl.BlockSpec((None, qb, LANES), lambda b, p, i: (b, i, p)),
        out_shape=jax.ShapeDtypeStruct((bsz, seq, d), BF16),
        scratch_shapes=[pltpu.VMEM((n_win, LANES, SWA_WINDOW), BF16)],
        compiler_params=_params("parallel", "parallel", "arbitrary"),
        name="swa_attn",
    )(_alibi_slopes(), sinks, qkv, qkv, qkv)


def _log_sigmoid(z):
    return jnp.minimum(z, 0.0) - jnp.log(1.0 + jnp.exp(-jnp.abs(z)))


def _suffix_sums(lk, upper):
    hi = lk.astype(BF16)
    lo = (lk - hi.astype(F32)).astype(BF16)
    n = lk.shape[0]
    both = jnp.dot(jnp.concatenate([hi, lo], axis=0), upper, preferred_element_type=F32)
    return both[:n] + both[n:]


def _sb_tile(qh, kb, vb, upper, later, strict):
    z = lax.dot_general(qh, kb, NT_DIMS, preferred_element_type=F32)
    ls = _log_sigmoid(z)
    lk = ls - z
    if strict is not None:
        lk = jnp.where(strict, lk, 0.0)
    after = _suffix_sums(lk, upper)
    if later is not None:
        after = after + later
    a = jnp.exp(ls + after)
    if strict is not None:
        a = jnp.where(strict, a, 0.0)
    mass = jnp.sum(lk, axis=1, keepdims=True)
    return jnp.dot(a.astype(BF16), vb, preferred_element_type=F32), (mass if later is None else later + mass)


def _sb_kernel(q_ref, k_ref, v_ref, o_ref, *, scale):
    c = pl.program_id(2)
    blk = SB_TILE
    row = lax.broadcasted_iota(jnp.int32, (blk, blk), 0)
    col = lax.broadcasted_iota(jnp.int32, (blk, blk), 1)
    strict = col < row
    upper = (row > col).astype(BF16)
    own0 = pl.multiple_of(c * blk, blk)
    k_own = k_ref[pl.ds(own0, blk), :]
    v_own = v_ref[pl.ds(own0, blk), :]
    q2 = q_ref[...]
    masks = _head_masks()
    qhs = [jnp.where(hmask, q2, jnp.zeros_like(q2)) * jnp.asarray(scale, BF16) for hmask in masks]

    carry = []
    for qh in qhs:
        acc, later = _sb_tile(qh, k_own, v_own, upper, None, strict)
        carry += [later, acc]

    def earlier_block(state):
        step, carry = state[0], state[1:]
        j0 = pl.multiple_of((c - 1 - step) * blk, blk)
        kb = k_ref[pl.ds(j0, blk), :]
        vb = v_ref[pl.ds(j0, blk), :]
        new = [step + 1]
        for hh, qh in enumerate(qhs):
            contrib, later = _sb_tile(qh, kb, vb, upper, carry[2 * hh], None)
            new += [later, carry[2 * hh + 1] + contrib]
        return tuple(new)

    def weights_can_be_nonzero(state):
        alive = jnp.maximum(jnp.max(state[1]), jnp.max(state[3])) >= SB_DEAD_MASS
        return (state[0] < c) & alive

    carry = lax.while_loop(weights_can_be_nonzero, earlier_block, (jnp.int32(0),) + tuple(carry))[1:]
    o_ref[...] = jnp.where(masks[0], carry[1], carry[3]).astype(o_ref.dtype)


def _sb_attention(qkv, d):
    bsz, seq, _ = qkv.shape
    blk = min(SB_TILE, seq)
    assert seq % blk == 0 and blk == SB_TILE
    ncol = d // LANES
    return pl.pallas_call(
        functools.partial(_sb_kernel, scale=HEAD_DIM ** -0.5),
        grid=(bsz, N_PAIRS, seq // blk),
        in_specs=[pl.BlockSpec((None, blk, LANES), lambda b, p, i: (b, i, p)),
                  pl.BlockSpec((None, seq, LANES), lambda b, p, i: (b, 0, ncol + p)),
                  pl.BlockSpec((None, seq, LANES), lambda b, p, i: (b, 0, 2 * ncol + p))],
        out_specs=pl.BlockSpec((None, blk, LANES), lambda b, p, i: (b, i, p)),
        out_shape=jax.ShapeDtypeStruct((bsz, seq, d), BF16),
        compiler_params=_params("parallel", "parallel", "arbitrary"),
        name="sb_attn",
    )(qkv, qkv, qkv)


def _split_bf16(a):
    hi = a.astype(BF16)
    return hi, (a - hi.astype(F32)).astype(BF16)


def _route(x, w, b, ids_ref, gates_ref):
    xh, xl = _split_bf16(x)
    wh, wl = _split_bf16(w)
    logits = (lax.dot_general(wh, xh, NT_DIMS, preferred_element_type=F32)
              + lax.dot_general(wh, xl, NT_DIMS, preferred_element_type=F32)
              + lax.dot_general(wl, xh, NT_DIMS, preferred_element_type=F32)) + b
    tm = logits.shape[1]
    gl = logits[0:N_GROUPS, :]
    g_row = lax.broadcasted_iota(jnp.int32, (N_GROUPS, tm), 0).astype(F32)
    g_max = jnp.max(gl, axis=0, keepdims=True)
    g_p = 1.0 / jnp.sum(jnp.exp(gl - g_max), axis=0, keepdims=True)
    g_idx = jnp.min(jnp.where(gl == g_max, g_row, float(N_GROUPS)), axis=0, keepdims=True)

    in_grp = jnp.zeros((EXPERTS_PER_GROUP, tm), F32)
    for g in range(N_GROUPS):
        lo = 8 + g * EXPERTS_PER_GROUP
        in_grp = jnp.where(g_idx == float(g), logits[lo:lo + EXPERTS_PER_GROUP, :], in_grp)
    e_exp = jnp.exp(in_grp - jnp.max(in_grp, axis=0, keepdims=True))
    e_p = e_exp / jnp.sum(e_exp, axis=0, keepdims=True)
    e_row = lax.broadcasted_iota(jnp.int32, (EXPERTS_PER_GROUP, tm), 0).astype(F32)
    p1 = jnp.max(e_p, axis=0, keepdims=True)
    i1 = jnp.min(jnp.where(e_p == p1, e_row, float(EXPERTS_PER_GROUP)), axis=0, keepdims=True)
    rest = jnp.where(e_row == i1, -1.0, e_p)
    p2 = jnp.max(rest, axis=0, keepdims=True)
    i2 = jnp.min(jnp.where(rest == p2, e_row, float(EXPERTS_PER_GROUP)), axis=0, keepdims=True)
    den = p1 + p2
    base = g_idx * float(EXPERTS_PER_GROUP)
    ids_ref[0:1, :] = (base + i1).astype(jnp.int32)
    ids_ref[1:2, :] = (base + i2).astype(jnp.int32)
    gates_ref[0:1, :] = g_p * (p1 / den)
    gates_ref[1:2, :] = g_p * (p2 / den)


def _sc_gather_rows(table, idx):
    n = idx.shape[0]
    width = table.shape[1]
    mesh = plsc.VectorSubcoreMesh(core_axis_name="core", subcore_axis_name="subcore")

    @pl.kernel(out_type=jax.ShapeDtypeStruct((n, width), table.dtype), mesh=mesh, scratch_types=[])
    def gather(table_hbm, idx_hbm, out_hbm):
        def step(idx_vmem, out_vmem):
            pltpu.sync_copy(table_hbm.at[idx_vmem.at[0]], out_vmem)

        pltpu.emit_pipeline(
            step,
            grid=(n // SC_WINDOW,),
            in_specs=[pl.BlockSpec((1, SC_WINDOW), index_map=lambda i: (0, i))],
            out_specs=[pl.BlockSpec((SC_WINDOW, width), index_map=lambda i: (i, 0))],
            core_axis_name=("core", "subcore"),
            dimension_semantics=(pltpu.PARALLEL,),
        )(idx_hbm, out_hbm)

    return gather(table, idx.reshape(1, n))


def _gather_slabs(slabs, rows):
    n_slab, n_rows, width = slabs.shape
    idx = (jnp.arange(n_slab, dtype=jnp.int32)[:, None] * n_rows + rows[None, :]).reshape(-1)
    return _sc_gather_rows(slabs.reshape(n_slab * n_rows, width), idx).reshape(n_slab, rows.shape[0], width)


def _scatter_slabs(slabs, rows, n_out):
    n_slab, n_tok, width = slabs.shape
    n = n_slab * rows.shape[0]
    tok_blocks = n_tok // SC_WINDOW
    idx = (jnp.arange(n_slab, dtype=jnp.int32)[:, None] * n_out + rows[None, :]).reshape(1, n)
    mesh = plsc.VectorSubcoreMesh(core_axis_name="core", subcore_axis_name="subcore")

    @pl.kernel(out_type=jax.ShapeDtypeStruct((n_slab * n_out, width), slabs.dtype), mesh=mesh, scratch_types=[])
    def scatter(src_hbm, idx_hbm, out_hbm):
        def step(src_vmem, idx_vmem):
            pltpu.sync_copy(src_vmem, out_hbm.at[idx_vmem.at[0]])

        pltpu.emit_pipeline(
            step,
            grid=(n // SC_WINDOW,),
            in_specs=[pl.BlockSpec((SC_WINDOW, width),
                                   index_map=lambda i: ((i // (2 * tok_blocks)) * tok_blocks + i % tok_blocks, 0)),
                      pl.BlockSpec((1, SC_WINDOW), index_map=lambda i: (0, i))],
            out_specs=[],
            core_axis_name=("core", "subcore"),
            dimension_semantics=(pltpu.PARALLEL,),
        )(src_hbm, idx_hbm)

    return scatter(slabs.reshape(n_slab * n_tok, width), idx).reshape(n_slab, n_out, width)


def _expert_kernel(block_e_ref, valid_ref, nused_ref, x_ref, wg_ref, wu_ref, wd_ref, y_ref, wg_bf, wu_bf, wd_bf):
    i = pl.program_id(0)

    @pl.when(i < nused_ref[0])
    def _():
        @pl.when((i == 0) | (block_e_ref[i] != block_e_ref[jnp.maximum(i - 1, 0)]))
        def _():
            wg_bf[...] = wg_ref[...].astype(BF16)
            wu_bf[...] = wu_ref[...].astype(BF16)
            wd_bf[...] = wd_ref[...].astype(BF16)

        xb = _from_packed_slabs(x_ref, valid_ref[i])
        g = jnp.dot(xb, wg_bf[...], preferred_element_type=F32)
        u = jnp.dot(xb, wu_bf[...], preferred_element_type=F32)
        h = (g * jax.nn.sigmoid(g)) * u
        _to_slabs(y_ref, jnp.dot(h.astype(BF16), wd_bf[...], preferred_element_type=F32))


def _expert_ffn(block_e, valid, nused, xs, wg, wu, wd, layer):
    n_packed, n_rows, width = xs.shape
    n_slab = N_SLABS
    d, de = wg.shape[1], wg.shape[2]
    n_blocks = n_rows // MOE_BLOCK
    base = layer * N_EXPERTS

    def rows(i, be, nv, nu):
        return (0, jnp.minimum(i, nu[0] - 1), 0)

    def expert(i, be, nv, nu):
        return (base + be[i], 0, 0)

    return pl.pallas_call(
        _expert_kernel,
        grid_spec=pltpu.PrefetchScalarGridSpec(
            num_scalar_prefetch=3,
            grid=(n_blocks,),
            in_specs=[pl.BlockSpec((n_packed, MOE_BLOCK, width), rows),
                      pl.BlockSpec((None, d, de), expert),
                      pl.BlockSpec((None, d, de), expert),
                      pl.BlockSpec((None, de, d), expert)],
            out_specs=pl.BlockSpec((n_slab, MOE_BLOCK, width), rows),
            scratch_shapes=[pltpu.VMEM((d, de), BF16), pltpu.VMEM((d, de), BF16), pltpu.VMEM((de, d), BF16)]),
        out_shape=jax.ShapeDtypeStruct((n_slab, n_rows, width), F32),
        compiler_params=_params("arbitrary"),
        name="moe_experts",
    )(block_e, valid, nused, xs, wg, wu, wd)


def _plan_kernel(e_ref, row_ref, block_e_ref, valid_ref, nused_ref):
    e_all = e_ref[...]
    n_chunk = e_all.shape[0]
    lane_before = (lax.broadcasted_iota(jnp.int32, (LANES, LANES), 0)
                   < lax.broadcasted_iota(jnp.int32, (LANES, LANES), 1))
    within = jnp.concatenate([lane_before.astype(BF16), jnp.ones((LANES, LANES), BF16)], axis=1)
    chunk_before = (lax.broadcasted_iota(jnp.int32, (n_chunk, n_chunk), 1)
                    < lax.broadcasted_iota(jnp.int32, (n_chunk, n_chunk), 0)).astype(BF16)
    block_lane = lax.broadcasted_iota(jnp.int32, block_e_ref.shape, 1).astype(F32)

    row = jnp.zeros(e_all.shape, F32)
    blocks_done = jnp.zeros((1, LANES), F32)
    experts_done = jnp.zeros(block_e_ref.shape, F32)
    valid = jnp.zeros(block_e_ref.shape, F32)
    for e in range(N_EXPERTS):
        mine = e_all == e
        counts = jnp.dot(mine.astype(BF16), within, preferred_element_type=F32)
        in_chunk, chunk_total = counts[:, :LANES], counts[:, LANES:]
        earlier_chunks = jnp.dot(chunk_before, chunk_total.astype(BF16), preferred_element_type=F32)
        n_mine = earlier_chunks[n_chunk - 1:, :] + chunk_total[n_chunk - 1:, :]
        row = jnp.where(mine, blocks_done * MOE_BLOCK + earlier_chunks + in_chunk, row)
        left = n_mine[:, 0:1] - (block_lane - blocks_done[:, 0:1]) * MOE_BLOCK
        valid = jnp.where(block_lane >= blocks_done[:, 0:1], jnp.clip(left, 0.0, float(MOE_BLOCK)), valid)
        blocks_done = blocks_done + jnp.floor((n_mine + (MOE_BLOCK - 1)) * (1.0 / MOE_BLOCK))
        experts_done = experts_done + (block_lane >= blocks_done[:, 0:1]).astype(F32)
    row_ref[...] = row.astype(jnp.int32)
    block_e_ref[...] = jnp.minimum(experts_done, N_EXPERTS - 1).astype(jnp.int32)
    valid_ref[...] = valid.astype(jnp.int32)
    nused_ref[...] = blocks_done.astype(jnp.int32)


def _dispatch_plan(ids, n_tok):
    n_asg = 2 * n_tok
    n_blocks = n_asg // MOE_BLOCK + N_EXPERTS
    block_lanes = -(-n_blocks // LANES) * LANES
    per_block = jax.ShapeDtypeStruct((1, block_lanes), jnp.int32)
    rows, block_e, valid, nused = pl.pallas_call(
        _plan_kernel,
        out_shape=[jax.ShapeDtypeStruct((n_asg // LANES, LANES), jnp.int32), per_block, per_block,
                   jax.ShapeDtypeStruct((1, LANES), jnp.int32)],
        compiler_params=pltpu.CompilerParams(vmem_limit_bytes=VMEM_LIMIT),
        name="moe_plan",
    )(ids.reshape(n_asg // LANES, LANES))
    return block_e[0, :n_blocks], valid[0, :n_blocks], nused[0, :1], rows.reshape(n_asg)


def _moe_outputs(x_slabs, ids, wg, wu, wd, layer):
    block_e, valid, nused, asg_row = _dispatch_plan(ids, ids.shape[1])
    xs = _scatter_slabs(x_slabs, asg_row, block_e.shape[0] * MOE_BLOCK)
    ys = _expert_ffn(block_e, valid, nused, xs, wg, wu, wd, layer)
    return _gather_slabs(ys, asg_row)


def _router_weights(w_grp, b_grp, w_exp, b_exp):
    d = w_grp.shape[0]
    pad = 8 - N_GROUPS
    w_rt = jnp.concatenate([w_grp.T, jnp.zeros((pad, d), F32), w_exp.T], axis=0)
    b_rt = jnp.concatenate([b_grp, jnp.zeros((pad,), F32), b_exp]).reshape(ROUTER_ROWS, 1)
    return w_rt, b_rt


def kernel(x, p, w_qkv_a, w_o_a, w_qkv_b, w_o_b, sinks_b, w_qkv_c, w_o_c, ln1_g, ln1_b, ln2_g, ln2_b, w_grp, b_grp, w_exp, b_exp, w_e_gate, w_e_up, w_e_down, w_ple_gate, w_ple_proj):
    bsz, seq, d = x.shape
    depth = p.shape[0]
    n_tok = bsz * seq
    alpha = (2.0 * depth) ** 0.25
    kv_w = SWA_KV_HEADS * HEAD_DIM
    swa_cols = jnp.asarray(np.concatenate([np.arange(h * HEAD_DIM, (h + 1) * HEAD_DIM) for h in _swa_head_order()]))

    n_all = depth * N_EXPERTS
    wg_all = w_e_gate.reshape(n_all, d, -1)
    wu_all = w_e_up.reshape(n_all, d, -1)
    wd_all = w_e_down.reshape(n_all, -1, d)
    p_all = p.reshape(depth * n_tok, -1)

    xt = x.reshape(n_tok, d)
    for i in range(depth):
        mixer, j = i % 3, i // 3
        if mixer == 0:
            qkv = _proj(xt, w_qkv_a[j].astype(BF16))
            o = _moba_attention(qkv.reshape(bsz, seq, 3 * d), d)
            w_o = w_o_a[j]
        elif mixer == 1:
            w_qkv = jnp.concatenate([w_qkv_b[j][:, :d][:, swa_cols], w_qkv_b[j][:, d:]], axis=1)
            qkv = _proj(xt, w_qkv.astype(BF16))
            o = _swa_attention(qkv.reshape(bsz, seq, d + 2 * kv_w), sinks_b[j], d)
            w_o = w_o_b[j][swa_cols, :]
        else:
            qkv = _proj(xt, w_qkv_c[j].astype(BF16))
            o = _sb_attention(qkv.reshape(bsz, seq, 3 * d), d)
            w_o = w_o_c[j]
        w_rt, b_rt = _router_weights(w_grp[i], b_grp[i], w_exp[i], b_exp[i])
        x1, x1_slabs, ids, gates = _oproj_ln(o.reshape(n_tok, d), w_o.astype(BF16), xt, ln1_g[i], ln1_b[i],
                                             w_rt, b_rt, alpha)
        m2 = _moe_outputs(x1_slabs, ids, wg_all, wu_all, wd_all, i)
        w0 = jnp.broadcast_to(gates[0][:, None], (n_tok, LANES))
        w1 = jnp.broadcast_to(gates[1][:, None], (n_tok, LANES))
        xt = _ln_ple(x1, m2, w0, w1, ln2_g[i], ln2_b[i], p_all, i,
                     w_ple_gate[i].astype(BF16), w_ple_proj[i].astype(BF16), alpha)
    return xt.reshape(bsz, seq, d)
```
